```python
import jax, jax.numpy as jnp
from jax import lax
import numpy as np

D_MODEL = 1024
BATCH = 2
SEQ = 16384
DEPTH = 2
DEC_BATCH = 16
DEC_SEQ = 32
PAST_LEN = 4096

CHUNK = 64
N_MIXERS = 2
N_LRU_LAYERS = (DEPTH + 1) // 2
N_FOX_LAYERS = DEPTH // 2
LRU_WIDTH = 3 * D_MODEL // 2
LRU_BLOCKS = 12
LRU_BLOCK_W = LRU_WIDTH // LRU_BLOCKS
CONV_W = 4
LRU_C = 8.0
FOX_HEADS = 16
FOX_HEAD_DIM = D_MODEL // FOX_HEADS
FOX_WIDTH = FOX_HEADS * FOX_HEAD_DIM
FOX_SCALE = FOX_HEAD_DIM ** -0.5
Q_BLOCK = 128
EPS = 1e-6

kernel_name = "hybrid_rglru_fox_stream_step"


def _rmsnorm(x, g):
    xf = x.astype(jnp.float32)
    y = xf * lax.rsqrt(jnp.mean(xf * xf, axis=-1, keepdims=True) + EPS)
    return (y * g.astype(jnp.float32)).astype(x.dtype)


def _adaln(c, w, b):
    mod = jax.nn.silu(c) @ w + b
    shift, scale, gate = jnp.split(mod[:, None, :], 3, axis=-1)
    return shift, scale, gate


def _lru_combine(e1, e2):
    a1, b1 = e1
    a2, b2 = e2
    return a1 * a2, a2 * b1 + b2


def _rglru_branch(h, conv_buf, h0, w_in, conv_w, conv_b, w_a, b_a, w_x, b_x, lam, w_out):
    B, T, _ = h.shape
    xb, gate = jnp.split(h @ w_in, 2, axis=-1)
    xp = jnp.concatenate([conv_buf.astype(xb.dtype), xb], axis=1)
    xc = conv_b + xp[:, 0:T] * conv_w[0]
    for k in range(1, CONV_W):
        xc = xc + xp[:, k:k + T] * conv_w[k]
    xblk = xc.reshape(B, T, LRU_BLOCKS, LRU_BLOCK_W)
    r = jax.nn.sigmoid(jnp.einsum("btnd,nde->btne", xblk, w_a).reshape(B, T, LRU_WIDTH) + b_a)
    ig = jax.nn.sigmoid(jnp.einsum("btnd,nde->btne", xblk, w_x).reshape(B, T, LRU_WIDTH) + b_x)
    log_a = -LRU_C * r.astype(jnp.float32) * jax.nn.softplus(-lam.astype(jnp.float32))
    a = jnp.exp(log_a)
    u = jnp.sqrt(-jnp.expm1(2.0 * log_a)) * (ig * xc).astype(jnp.float32)
    a_cum, u_cum = lax.associative_scan(_lru_combine, (a, u), axis=1)
    hs = a_cum * h0[:, None, :].astype(jnp.float32) + u_cum
    y = (hs.astype(h.dtype) * jax.nn.silu(gate)) @ w_out
    return y, xp[:, T:], hs[:, -1]


def _fox_attend_block(q, cq, qpos, k, v, ck, kpos):
    s = jnp.einsum("bqhd,bkhd->bhqk", q, k).astype(jnp.float32) * FOX_SCALE
    s = s + jnp.swapaxes(cq, 1, 2)[..., :, None] - jnp.swapaxes(ck, 1, 2)[..., None, :]
    s = jnp.where(kpos[None, None, None, :] <= qpos[None, None, :, None], s, -jnp.inf)
    p = jax.nn.softmax(s, axis=-1)
    return jnp.einsum("bhqk,bkhd->bqhd", p.astype(v.dtype), v)


def _fox_branch(h, past_k, past_v, past_lf, w_in, b_f, w_out):
    B, T, _ = h.shape
    P = past_k.shape[1]
    W = FOX_WIDTH
    q, k, v, g, fl = jnp.split(h @ w_in, [W, 2 * W, 3 * W, 4 * W], axis=-1)
    q = q.reshape(B, T, FOX_HEADS, FOX_HEAD_DIM)
    k = k.reshape(B, T, FOX_HEADS, FOX_HEAD_DIM)
    v = v.reshape(B, T, FOX_HEADS, FOX_HEAD_DIM)
    log_f = jax.nn.log_sigmoid(fl.astype(jnp.float32) + b_f.astype(jnp.float32))
    k_all = jnp.concatenate([past_k.astype(k.dtype), k], axis=1)
    v_all = jnp.concatenate([past_v.astype(v.dtype), v], axis=1)
    cum = jnp.cumsum(jnp.concatenate([past_lf.astype(jnp.float32), log_f], axis=1), axis=1)
    kpos = jnp.arange(P + T)
    qb = Q_BLOCK if T % Q_BLOCK == 0 else T
    nb = T // qb
    qs = jnp.swapaxes(q.reshape(B, nb, qb, FOX_HEADS, FOX_HEAD_DIM), 0, 1)
    cqs = jnp.swapaxes(cum[:, P:].reshape(B, nb, qb, FOX_HEADS), 0, 1)
    qposs = (P + jnp.arange(T)).reshape(nb, qb)
    o = lax.map(lambda blk: _fox_attend_block(blk[0], blk[1], blk[2], k_all, v_all, cum, kpos),
                (qs, cqs, qposs))
    o = jnp.swapaxes(o, 0, 1).reshape(B, T, W)
    y = (o * jax.nn.silu(g)) @ w_out
    return y, k, v, log_f


def setup_inputs(seed: int = 0) -> dict:
    key = jax.random.key(seed)
    ks = jax.random.split(key, 32)
    f32 = jnp.float32
    R, W, H, dh = LRU_WIDTH, FOX_WIDTH, FOX_HEADS, FOX_HEAD_DIM
    NA, NF = N_LRU_LAYERS, N_FOX_LAYERS

    def nrm(k, shape, s):
        return s * jax.random.normal(k, shape, f32)

    a0 = jax.random.uniform(ks[20], (NA, R), f32, 0.9, 0.999) ** (1.0 / LRU_C)
    lru_lambda = jnp.log(a0) - jnp.log1p(-a0)
    return {
        "x_prompt": nrm(ks[0], (BATCH, SEQ, D_MODEL), 1.0),
        "x_sample": nrm(ks[1], (DEC_BATCH, DEC_SEQ, D_MODEL), 1.0),
        "c_prompt": nrm(ks[2], (BATCH, D_MODEL), 1.0),
        "c_sample": nrm(ks[3], (DEC_BATCH, D_MODEL), 1.0),
        "state_lru_h": nrm(ks[4], (NA, DEC_BATCH, R), 0.5),
        "state_lru_conv": nrm(ks[5], (NA, DEC_BATCH, CONV_W - 1, R), 1.0),
        "cache_fox_k": nrm(ks[6], (NF, DEC_BATCH, PAST_LEN, H, dh), 1.0),
        "cache_fox_v": nrm(ks[7], (NF, DEC_BATCH, PAST_LEN, H, dh), 1.0),
        "cache_fox_logf": jax.nn.log_sigmoid(3.0 + jax.random.normal(ks[8], (NF, DEC_BATCH, PAST_LEN, H), f32)),
        "norm_pre": 1.0 + nrm(ks[9], (DEPTH, D_MODEL), 0.05),
        "norm_post": 1.0 + nrm(ks[10], (DEPTH, D_MODEL), 0.05),
        "ada_w": nrm(ks[11], (DEPTH, D_MODEL, 3 * D_MODEL), 0.5 * D_MODEL ** -0.5),
        "ada_b": nrm(ks[12], (DEPTH, 3 * D_MODEL), 0.01),
        "lru_w_in": nrm(ks[13], (NA, D_MODEL, 2 * R), D_MODEL ** -0.5),
        "lru_conv_w": nrm(ks[14], (NA, CONV_W, R), CONV_W ** -0.5),
        "lru_conv_b": nrm(ks[15], (NA, R), 0.01),
        "lru_w_a": nrm(ks[16], (NA, LRU_BLOCKS, LRU_BLOCK_W, LRU_BLOCK_W), LRU_BLOCK_W ** -0.5),
        "lru_b_a": nrm(ks[17], (NA, R), 0.01),
        "lru_w_x": nrm(ks[18], (NA, LRU_BLOCKS, LRU_BLOCK_W, LRU_BLOCK_W), LRU_BLOCK_W ** -0.5),
        "lru_b_x": nrm(ks[19], (NA, R), 0.01),
        "lru_lambda": lru_lambda,
        "lru_w_out": nrm(ks[21], (NA, R, D_MODEL), R ** -0.5),
        "fox_w_in": nrm(ks[22], (NF, D_MODEL, 4 * W + H), D_MODEL ** -0.5),
        "fox_b_f": jax.random.uniform(ks[23], (NF, H), f32, 2.0, 5.0),
        "fox_w_out": nrm(ks[24], (NF, W, D_MODEL), W ** -0.5),
    }


def reference(x_prompt, x_sample, c_prompt, c_sample, state_lru_h, state_lru_conv,
              cache_fox_k, cache_fox_v, cache_fox_logf, norm_pre, norm_post, ada_w, ada_b,
              lru_w_in, lru_conv_w, lru_conv_b, lru_w_a, lru_b_a, lru_w_x, lru_b_x,
              lru_lambda, lru_w_out, fox_w_in, fox_b_f, fox_w_out):
    B = x_prompt.shape[0]
    xp, xs = x_prompt, x_sample
    lru_h_p, lru_c_p, lru_h_s, lru_c_s = [], [], [], []
    fk_p, fv_p, ff_p, fk_s, fv_s, ff_s = [], [], [], [], [], []
    for i in range(DEPTH):
        j = i // N_MIXERS
        sh_p, sc_p, gt_p = _adaln(c_prompt, ada_w[i], ada_b[i])
        sh_s, sc_s, gt_s = _adaln(c_sample, ada_w[i], ada_b[i])
        hp = _rmsnorm(xp, norm_pre[i]) * (1.0 + sc_p) + sh_p
        hs = _rmsnorm(xs, norm_pre[i]) * (1.0 + sc_s) + sh_s
        if i % N_MIXERS == 0:
            lw = (lru_w_in[j], lru_conv_w[j], lru_conv_b[j], lru_w_a[j], lru_b_a[j],
                  lru_w_x[j], lru_b_x[j], lru_lambda[j], lru_w_out[j])
            yp, cbp, hlp = _rglru_branch(hp, jnp.zeros((B, CONV_W - 1, LRU_WIDTH), hp.dtype),
                                         jnp.zeros((B, LRU_WIDTH), jnp.float32), *lw)
            ys, cbs, hls = _rglru_branch(hs, state_lru_conv[j], state_lru_h[j], *lw)
            lru_h_p.append(hlp)
            lru_c_p.append(cbp)
            lru_h_s.append(hls)
            lru_c_s.append(cbs)
        else:
            fw = (fox_w_in[j], fox_b_f[j], fox_w_out[j])
            yp, kp, vp, lfp = _fox_branch(
                hp, jnp.zeros((B, 0, FOX_HEADS, FOX_HEAD_DIM), hp.dtype),
                jnp.zeros((B, 0, FOX_HEADS, FOX_HEAD_DIM), hp.dtype),
                jnp.zeros((B, 0, FOX_HEADS), jnp.float32), *fw)
            ys, ksn, vsn, lfs = _fox_branch(hs, cache_fox_k[j], cache_fox_v[j], cache_fox_logf[j], *fw)
            fk_p.append(kp)
            fv_p.append(vp)
            ff_p.append(lfp)
            fk_s.append(ksn)
            fv_s.append(vsn)
            ff_s.append(lfs)
        xp = xp + gt_p * _rmsnorm(yp, norm_post[i])
        xs = xs + gt_s * _rmsnorm(ys, norm_post[i])
    return (xp, xs,
            jnp.stack(lru_h_p), jnp.stack(lru_c_p), jnp.stack(fk_p), jnp.stack(fv_p), jnp.stack(ff_p),
            jnp.stack(lru_h_s), jnp.stack(lru_c_s), jnp.stack(fk_s), jnp.stack(fv_s), jnp.stack(ff_s))
```

```python
import functools

import jax
import jax.numpy as jnp
from jax import lax
from jax.experimental import pallas as pl
from jax.experimental.pallas import tpu as pltpu

F32 = jnp.float32
BF16 = jnp.bfloat16

EPS = 1e-6
LRU_C = 8.0
CONV_TAPS = 4
HEAD_DIM = 64
LANES = 128
SUBLANES = 8
NEG_BIG = -1e30
VMEM_LIMIT_BYTES = 56 * 1024 * 1024

_NT = (((1,), (1,)), ((), ()))


def _params(n_grid):
    return pltpu.CompilerParams(
        dimension_semantics=("arbitrary",) * n_grid,
        vmem_limit_bytes=VMEM_LIMIT_BYTES)


def _const_spec(shape):
    nd = len(shape)
    return pl.BlockSpec(shape, lambda *_: (0,) * nd, pipeline_mode=pl.Buffered(1))


def _silu(x):
    return x * jax.nn.sigmoid(x)


def _softplus(x):
    return jnp.maximum(x, 0.0) + jnp.log1p(jnp.exp(-jnp.abs(x)))


def _one_minus_exp(y):
    p = jnp.full_like(y, 1.0 / 5040.0)
    for c in (1.0 / 720.0, 1.0 / 120.0, 1.0 / 24.0, 1.0 / 6.0, 0.5, 1.0):
        p = p * y + c
    return jnp.where(y > -0.25, -y * p, 1.0 - jnp.exp(y))


def _modulated_norm(x, mod, g):
    ms = jnp.mean(x * x, axis=-1, keepdims=True)
    h = x * lax.rsqrt(ms + EPS) * g
    return h * (1.0 + mod[1:2]) + mod[0:1]


def _gated_residual(x, y, mod, g):
    ms = jnp.mean(y * y, axis=-1, keepdims=True)
    return x + mod[2:3] * (y * lax.rsqrt(ms + EPS) * g)


def _adaln_kernel(c_ref, w_ref, b_ref, o_ref):
    sc = _silu(c_ref[...]).astype(BF16)
    o_ref[0] = jnp.dot(sc, w_ref[0].astype(BF16), preferred_element_type=F32) + b_ref[0]


def _adaln(c_all, ada_w, ada_b):
    depth, d, d3 = ada_w.shape
    n = c_all.shape[0]
    nj = d3 // d
    return pl.pallas_call(
        _adaln_kernel,
        grid=(depth, nj),
        in_specs=[
            pl.BlockSpec((n, d), lambda i, j: (0, 0)),
            pl.BlockSpec((1, d, d), lambda i, j: (i, 0, j)),
            pl.BlockSpec((1, 1, d), lambda i, j: (i, 0, j)),
        ],
        out_specs=pl.BlockSpec((1, n, d), lambda i, j: (i, 0, j)),
        out_shape=jax.ShapeDtypeStruct((depth, n, d3), F32),
        compiler_params=_params(2),
        name="adaln",
    )(c_all, ada_w, ada_b.reshape(depth, 1, d3))


def _lru_kernel(x_ref, mod_ref, gpre_ref, gpost_ref, win_ref, cw_ref, cb_ref, wax_ref,
                ba_ref, bx_ref, lam_ref, wout_ref, conv0_ref, h0_ref,
                y_ref, convo_ref, hlast_ref,
                xbuf, xc_s, pre_a, pre_x, gate_s, yv_s, hcar, *, tt, r, nblk):
    t = pl.program_id(1)
    nt = pl.num_programs(1)
    tail0 = SUBLANES - (CONV_TAPS - 1)

    @pl.when(t == 0)
    def _():
        xbuf[tail0:SUBLANES, :] = conv0_ref[0]
        hcar[...] = h0_ref[0]

    x = x_ref[0]
    mod = mod_ref[0]
    h = _modulated_norm(x, mod, gpre_ref[...])
    z = jnp.dot(h.astype(BF16), win_ref[...], preferred_element_type=F32)
    xbuf[SUBLANES:SUBLANES + tt, :] = z[:, :r]
    gate_s[...] = z[:, r:]

    cw = cw_ref[...]
    xc = cb_ref[...] + xbuf[tail0:tail0 + tt, :] * cw[0:1]
    for k in range(1, CONV_TAPS):
        xc = xc + xbuf[tail0 + k:tail0 + k + tt, :] * cw[k:k + 1]
    tail = xbuf[tt + tail0:tt + SUBLANES, :]
    xbuf[tail0:SUBLANES, :] = tail
    xc_s[...] = xc

    @pl.when(t == nt - 1)
    def _():
        convo_ref[0] = tail

    xcb = xc.astype(BF16)
    for n in range(nblk):
        lo, hi = n * LANES, (n + 1) * LANES
        res = jnp.dot(xcb[:, lo:hi], wax_ref[n], preferred_element_type=F32)
        pre_a[:, lo:hi] = res[:, :LANES]
        pre_x[:, lo:hi] = res[:, LANES:]

    coef = -LRU_C * _softplus(-lam_ref[...])
    ba = ba_ref[...]
    bx = bx_ref[...]
    row = lax.broadcasted_iota(jnp.int32, (SUBLANES, r), 0)

    def group(g, hprev):
        st = pl.multiple_of(g * SUBLANES, SUBLANES)
        sl = pl.ds(st, SUBLANES)
        rg = jax.nn.sigmoid(pre_a[sl, :] + ba)
        ig = jax.nn.sigmoid(pre_x[sl, :] + bx)
        log_a = coef * rg
        a = jnp.exp(log_a)
        u = jnp.sqrt(_one_minus_exp(2.0 * log_a)) * (ig * xc_s[sl, :])
        for s in (1, 2, 4):
            a_sh = jnp.where(row >= s, pltpu.roll(a, s, axis=0), 1.0)
            u_sh = jnp.where(row >= s, pltpu.roll(u, s, axis=0), 0.0)
            u = a * u_sh + u
            a = a * a_sh
        hs = a * hprev + u
        yv_s[sl, :] = hs * _silu(gate_s[sl, :])
        return hs[SUBLANES - 1:SUBLANES, :]

    hlast = lax.fori_loop(0, tt // SUBLANES, group, hcar[...])
    hcar[...] = hlast

    @pl.when(t == nt - 1)
    def _():
        hlast_ref[0] = hlast

    y = jnp.dot(yv_s[...].astype(BF16), wout_ref[...], preferred_element_type=F32)
    y_ref[0] = _gated_residual(x, y, mod, gpost_ref[...])


def _lru_layer(x, mod, mod_off, gpre, gpost, w_in, conv_w, conv_b, w_ax, b_a, b_x, lam, w_out,
               conv0, h0, *, tt):
    b, t, d = x.shape
    r = lam.shape[-1]
    nblk = w_ax.shape[0]
    kern = functools.partial(_lru_kernel, tt=tt, r=r, nblk=nblk)
    return pl.pallas_call(
        kern,
        grid=(b, t // tt),
        in_specs=[
            pl.BlockSpec((1, tt, d), lambda i, j: (i, j, 0)),
            pl.BlockSpec((1, 3, d), lambda i, j: (i + mod_off, 0, 0)),
            _const_spec((1, d)), _const_spec((1, d)),
            _const_spec((d, 2 * r)),
            _const_spec((CONV_TAPS, r)), _const_spec((1, r)),
            _const_spec((nblk, LANES, 2 * LANES)),
            _const_spec((1, r)), _const_spec((1, r)), _const_spec((1, r)),
            _const_spec((r, d)),
            pl.BlockSpec((1, CONV_TAPS - 1, r), lambda i, j: (i, 0, 0)),
            pl.BlockSpec((1, 1, r), lambda i, j: (i, 0, 0)),
        ],
        out_specs=[
            pl.BlockSpec((1, tt, d), lambda i, j: (i, j, 0)),
            pl.BlockSpec((1, CONV_TAPS - 1, r), lambda i, j: (i, 0, 0)),
            pl.BlockSpec((1, 1, r), lambda i, j: (i, 0, 0)),
        ],
        out_shape=[
            jax.ShapeDtypeStruct((b, t, d), F32),
            jax.ShapeDtypeStruct((b, CONV_TAPS - 1, r), F32),
            jax.ShapeDtypeStruct((b, 1, r), F32),
        ],
        scratch_shapes=[
            pltpu.VMEM((tt + SUBLANES, r), F32),
            pltpu.VMEM((tt, r), F32), pltpu.VMEM((tt, r), F32), pltpu.VMEM((tt, r), F32),
            pltpu.VMEM((tt, r), F32), pltpu.VMEM((tt, r), F32),
            pltpu.VMEM((1, r), F32),
        ],
        compiler_params=_params(2),
        name="lru_layer",
    )(x, mod, gpre, gpost, w_in, conv_w, conv_b, w_ax, b_a, b_x, lam, w_out, conv0, h0)


def _fox_in_kernel(x_ref, mod_ref, gpre_ref, w_ref, wft_ref, bf_ref,
                   q_ref, kb_ref, vb_ref, k_ref, v_ref, g_ref, lft_ref, *, w, scale):
    hb = _modulated_norm(x_ref[0], mod_ref[0], gpre_ref[...]).astype(BF16)
    z = jnp.dot(hb, w_ref[...], preferred_element_type=F32)
    q_ref[0] = (z[:, :w] * scale).astype(BF16)
    k = z[:, w:2 * w]
    v = z[:, 2 * w:3 * w]
    k_ref[0] = k
    v_ref[0] = v
    kb_ref[0] = k.astype(BF16)
    vb_ref[0] = v.astype(BF16)
    g_ref[0] = z[:, 3 * w:]
    fl = lax.dot_general(wft_ref[...], hb, _NT, preferred_element_type=F32) + bf_ref[...]
    lft_ref[0] = -_softplus(-fl)


def _fox_in(x, mod, mod_off, gpre, w_main, w_ft, b_f, *, tt, scale):
    b, t, d = x.shape
    w = w_main.shape[1] // 4
    nh = w_ft.shape[0]
    kern = functools.partial(_fox_in_kernel, w=w, scale=scale)
    tile = lambda dt: jax.ShapeDtypeStruct((b, t, w), dt)
    tile_spec = pl.BlockSpec((1, tt, w), lambda i, j: (i, j, 0))
    return pl.pallas_call(
        kern,
        grid=(b, t // tt),
        in_specs=[
            pl.BlockSpec((1, tt, d), lambda i, j: (i, j, 0)),
            pl.BlockSpec((1, 3, d), lambda i, j: (i + mod_off, 0, 0)),
            _const_spec((1, d)),
            _const_spec((d, 4 * w)),
            _const_spec((nh, d)),
            _const_spec((nh, 1)),
        ],
        out_specs=[tile_spec] * 6 + [pl.BlockSpec((1, nh, tt), lambda i, j: (i, 0, j))],
        out_shape=[tile(BF16), tile(BF16), tile(BF16), tile(F32), tile(F32), tile(F32),
                   jax.ShapeDtypeStruct((b, nh, t), F32)],
        compiler_params=_params(2),
        name="fox_in",
    )(x, mod, gpre, w_main, w_ft, b_f)


def _cumsum_kernel(x_ref, u_ref, o_ref, car):
    @pl.when(pl.program_id(1) == 0)
    def _():
        car[...] = jnp.zeros_like(car)

    x = x_ref[0]
    u = u_ref[...]
    x1 = x.astype(BF16)
    r1 = x - x1.astype(F32)
    x2 = r1.astype(BF16)
    x3 = (r1 - x2.astype(F32)).astype(BF16)
    c = (jnp.dot(x1, u, preferred_element_type=F32)
         + jnp.dot(x2, u, preferred_element_type=F32)
         + jnp.dot(x3, u, preferred_element_type=F32)) + car[...]
    o_ref[0] = c
    tc = c.shape[1]
    car[...] = c[:, tc - 1:tc]


def _cumsum_time(x, *, tc):
    b, nh, t = x.shape
    u = jnp.triu(jnp.ones((tc, tc), F32)).astype(BF16)
    return pl.pallas_call(
        _cumsum_kernel,
        grid=(b, t // tc),
        in_specs=[pl.BlockSpec((1, nh, tc), lambda i, j: (i, 0, j)), _const_spec((tc, tc))],
        out_specs=pl.BlockSpec((1, nh, tc), lambda i, j: (i, 0, j)),
        out_shape=jax.ShapeDtypeStruct((b, nh, t), F32),
        scratch_shapes=[pltpu.VMEM((nh, 1), F32)],
        compiler_params=_params(2),
        name="cumsum_time",
    )(x, u)


def _split_heads(q):
    qf = q.astype(F32)
    lane = lax.broadcasted_iota(jnp.int32, qf.shape, 1)
    return (jnp.where(lane < HEAD_DIM, qf, 0.0).astype(BF16),
            jnp.where(lane >= HEAD_DIM, qf, 0.0).astype(BF16))


def _attn_kernel(q_ref, k_ref, v_ref, c_ref, o_ref, m_s, l_s, acc_s, *, tq):
    qi = pl.program_id(2)
    qh = _split_heads(q_ref[0])
    m_s[...] = jnp.full_like(m_s, NEG_BIG)
    l_s[...] = jnp.zeros_like(l_s)
    acc_s[...] = jnp.zeros_like(acc_s)
    reps = tq // LANES

    def step(j, masked):
        st = pl.multiple_of(j * tq, tq)
        kblk = k_ref[0, pl.ds(st, tq), :]
        vblk = v_ref[0, pl.ds(st, tq), :]
        for hh in range(2):
            s = lax.dot_general(qh[hh], kblk, _NT, preferred_element_type=F32)
            z = s - c_ref[0, 0, j, hh:hh + 1, :]
            if masked:
                rowi = lax.broadcasted_iota(jnp.int32, z.shape, 0)
                coli = lax.broadcasted_iota(jnp.int32, z.shape, 1)
                z = jnp.where(coli <= rowi, z, NEG_BIG)
            m_prev = m_s[hh]
            m_next = jnp.maximum(m_prev, jnp.max(z, axis=1)[:, None])
            p = jnp.exp(z - jnp.tile(m_next, (1, reps)))
            alpha = jnp.exp(m_prev - m_next)
            l_s[hh] = alpha * l_s[hh] + jnp.sum(p, axis=1)[:, None]
            m_s[hh] = m_next
            pv = jnp.dot(p.astype(BF16), vblk, preferred_element_type=F32)
            acc_s[hh] = acc_s[hh] * alpha + pv

    step(qi, True)

    def body(j, carry):
        step(j, False)
        return carry

    lax.fori_loop(0, qi, body, 0)
    lane = lax.broadcasted_iota(jnp.int32, (tq, LANES), 1)
    o_ref[0] = jnp.where(lane < HEAD_DIM, acc_s[0] / l_s[0], acc_s[1] / l_s[1])


def _attention(qb, kb, vb, cum, *, tq):
    b, t, w = qb.shape
    hp = w // LANES
    nq = t // tq
    c5 = cum.reshape(b, hp, 2, nq, tq).transpose(0, 1, 3, 2, 4)
    kern = functools.partial(_attn_kernel, tq=tq)
    return pl.pallas_call(
        kern,
        grid=(b, hp, nq),
        in_specs=[
            pl.BlockSpec((1, tq, LANES), lambda i, h, j: (i, j, h)),
            pl.BlockSpec((1, t, LANES), lambda i, h, j: (i, 0, h)),
            pl.BlockSpec((1, t, LANES), lambda i, h, j: (i, 0, h)),
            pl.BlockSpec((1, 1, nq, 2, tq), lambda i, h, j: (i, h, 0, 0, 0)),
        ],
        out_specs=pl.BlockSpec((1, tq, LANES), lambda i, h, j: (i, j, h)),
        out_shape=jax.ShapeDtypeStruct((b, t, w), F32),
        scratch_shapes=[pltpu.VMEM((2, tq, LANES), F32)] * 3,
        compiler_params=_params(3),
        name="fox_attention",
    )(qb, kb, vb, c5)


def _attn_cached_kernel(q_ref, kn_ref, vn_ref, pk_ref, pv_ref, cp_ref, cn_ref, o_ref):
    qh = _split_heads(q_ref[0])
    pkb = pk_ref[0].astype(BF16)
    pvb = pv_ref[0].astype(BF16)
    kn = kn_ref[0]
    vn = vn_ref[0]
    t = kn.shape[0]
    p_len = pkb.shape[0]
    rowi = lax.broadcasted_iota(jnp.int32, (t, t), 0)
    coli = lax.broadcasted_iota(jnp.int32, (t, t), 1)
    outs = []
    for hh in range(2):
        cp = cp_ref[0, 0, hh:hh + 1, :]
        tot = cp[:, p_len - 1:p_len]
        z1 = lax.dot_general(qh[hh], pkb, _NT, preferred_element_type=F32) - (cp - tot)
        z2 = lax.dot_general(qh[hh], kn, _NT, preferred_element_type=F32) - cn_ref[0, 0, hh:hh + 1, :]
        z2 = jnp.where(coli <= rowi, z2, NEG_BIG)
        m = jnp.maximum(jnp.max(z1, axis=1), jnp.max(z2, axis=1))[:, None]
        p1 = jnp.exp(z1 - m)
        p2 = jnp.exp(z2 - m)
        l = (jnp.sum(p1, axis=1) + jnp.sum(p2, axis=1))[:, None]
        o = (jnp.dot(p1.astype(BF16), pvb, preferred_element_type=F32)
             + jnp.dot(p2.astype(BF16), vn, preferred_element_type=F32))
        outs.append(o / l)
    lane = lax.broadcasted_iota(jnp.int32, (t, LANES), 1)
    o_ref[0] = jnp.where(lane < HEAD_DIM, outs[0], outs[1])


def _attention_cached(qb, kb, vb, past_k, past_v, cum_past, cum_new):
    b, t, w = qb.shape
    p_len = past_k.shape[1]
    hp = w // LANES
    new_spec = pl.BlockSpec((1, t, LANES), lambda i, h: (i, 0, h))
    past_spec = pl.BlockSpec((1, p_len, LANES), lambda i, h: (i, 0, h))
    return pl.pallas_call(
        _attn_cached_kernel,
        grid=(b, hp),
        in_specs=[new_spec, new_spec, new_spec, past_spec, past_spec,
                  pl.BlockSpec((1, 1, 2, p_len), lambda i, h: (i, h, 0, 0)),
                  pl.BlockSpec((1, 1, 2, t), lambda i, h: (i, h, 0, 0))],
        out_specs=new_spec,
        out_shape=jax.ShapeDtypeStruct((b, t, w), F32),
        compiler_params=_params(2),
        name="fox_attention_cached",
    )(qb, kb, vb, past_k, past_v,
      cum_past.reshape(b, hp, 2, p_len), cum_new.reshape(b, hp, 2, t))


def _fox_out_kernel(o_ref, g_ref, x_ref, mod_ref, gpost_ref, w_ref, y_ref):
    yv = (o_ref[0] * _silu(g_ref[0])).astype(BF16)
    y = jnp.dot(yv, w_ref[...], preferred_element_type=F32)
    y_ref[0] = _gated_residual(x_ref[0], y, mod_ref[0], gpost_ref[...])


def _fox_out(o, g, x, mod, mod_off, gpost, w_out, *, tt):
    b, t, d = x.shape
    w = o.shape[-1]
    return pl.pallas_call(
        _fox_out_kernel,
        grid=(b, t // tt),
        in_specs=[
            pl.BlockSpec((1, tt, w), lambda i, j: (i, j, 0)),
            pl.BlockSpec((1, tt, w), lambda i, j: (i, j, 0)),
            pl.BlockSpec((1, tt, d), lambda i, j: (i, j, 0)),
            pl.BlockSpec((1, 3, d), lambda i, j: (i + mod_off, 0, 0)),
            _const_spec((1, d)),
            _const_spec((w, d)),
        ],
        out_specs=pl.BlockSpec((1, tt, d), lambda i, j: (i, j, 0)),
        out_shape=jax.ShapeDtypeStruct((b, t, d), F32),
        compiler_params=_params(2),
        name="fox_out",
    )(o, g, x, mod, gpost, w_out)


def _tile(t, pref):
    return pref if t % pref == 0 else t


def kernel(x_prompt, x_sample, c_prompt, c_sample, state_lru_h, state_lru_conv, cache_fox_k, cache_fox_v, cache_fox_logf, norm_pre, norm_post, ada_w, ada_b, lru_w_in, lru_conv_w, lru_conv_b, lru_w_a, lru_b_a, lru_w_x, lru_b_x, lru_lambda, lru_w_out, fox_w_in, fox_b_f, fox_w_out):
    b, t, d = x_prompt.shape
    db, dt, _ = x_sample.shape
    depth = ada_w.shape[0]
    r = lru_lambda.shape[-1]
    nh = fox_b_f.shape[-1]
    w = fox_w_out.shape[1]
    p_len = cache_fox_k.shape[2]
    scale = float(HEAD_DIM) ** -0.5

    c_all = jnp.concatenate([c_prompt, c_sample], axis=0)
    mod = _adaln(c_all, ada_w, ada_b).reshape(depth, b + db, 3, d)

    xp, xs = x_prompt, x_sample

    j = 0
    w_ax = jnp.concatenate([lru_w_a[j], lru_w_x[j]], axis=-1).astype(BF16)
    lru_args = (norm_pre[0:1], norm_post[0:1], lru_w_in[j].astype(BF16), lru_conv_w[j],
                lru_conv_b[j:j + 1], w_ax, lru_b_a[j:j + 1], lru_b_x[j:j + 1],
                lru_lambda[j:j + 1], lru_w_out[j].astype(BF16))
    xp, conv_p, h_p = _lru_layer(
        xp, mod[0], 0, *lru_args,
        jnp.zeros((b, CONV_TAPS - 1, r), F32), jnp.zeros((b, 1, r), F32), tt=_tile(t, 256))
    xs, conv_s, h_s = _lru_layer(
        xs, mod[0], b, *lru_args,
        state_lru_conv[j], state_lru_h[j].reshape(db, 1, r), tt=_tile(dt, 256))

    w_main = fox_w_in[j][:, :4 * w].astype(BF16)
    w_ft = fox_w_in[j][:, 4 * w:].T.astype(BF16)
    b_f = fox_b_f[j].reshape(nh, 1)
    w_out = fox_w_out[j].astype(BF16)

    qb, kb, vb, k_p, v_p, g_p, lft_p = _fox_in(
        xp, mod[1], 0, norm_pre[1:2], w_main, w_ft, b_f, tt=_tile(t, 256), scale=scale)
    tq = _tile(t, 512)
    cum_p = _cumsum_time(lft_p, tc=tq)
    o_p = _attention(qb, kb, vb, cum_p, tq=tq)
    yp = _fox_out(o_p, g_p, xp, mod[1], 0, norm_post[1:2], w_out, tt=_tile(t, 512))

    qs, ksb, vsb, k_s, v_s, g_s, lft_s = _fox_in(
        xs, mod[1], b, norm_pre[1:2], w_main, w_ft, b_f, tt=_tile(dt, 256), scale=scale)
    cum_new = _cumsum_time(lft_s, tc=_tile(dt, 512))
    cum_past = _cumsum_time(jnp.swapaxes(cache_fox_logf[j], 1, 2), tc=_tile(p_len, 512))
    o_s = _attention_cached(qs, ksb, vsb, cache_fox_k[j].reshape(db, p_len, w),
                            cache_fox_v[j].reshape(db, p_len, w), cum_past, cum_new)
    ys = _fox_out(o_s, g_s, xs, mod[1], b, norm_post[1:2], w_out, tt=_tile(dt, 512))

    hd = (nh, HEAD_DIM)
    return (yp, ys,
            h_p.reshape(1, b, r), conv_p[None],
            k_p.reshape(1, b, t, *hd), v_p.reshape(1, b, t, *hd), jnp.swapaxes(lft_p, 1, 2)[None],
            h_s.reshape(1, db, r), conv_s[None],
            k_s.reshape(1, db, dt, *hd), v_s.reshape(1, db, dt, *hd), jnp.swapaxes(lft_s, 1, 2)[None])
```

```python
import functools

import jax
import jax.numpy as jnp
from jax import lax
from jax.experimental import pallas as pl
from jax.experimental.pallas import tpu as pltpu

F32 = jnp.float32
BF16 = jnp.bfloat16

EPS = 1e-6
LRU_C = 8.0
CONV_TAPS = 4
HEAD_DIM = 64
LANES = 128
SUBLANES = 8
NEG_BIG = -1e30
LOG2E = 1.4426950408889634
ONES_ROWS = 16
VMEM_LIMIT_BYTES = 56 * 1024 * 1024

_NT = (((1,), (1,)), ((), ()))


def _params(n_grid):
    return pltpu.CompilerParams(
        dimension_semantics=("arbitrary",) * n_grid,
        vmem_limit_bytes=VMEM_LIMIT_BYTES)


def _const_spec(shape):
    nd = len(shape)
    return pl.BlockSpec(shape, lambda *_: (0,) * nd, pipeline_mode=pl.Buffered(1))


def _silu(x):
    return x * jax.nn.sigmoid(x)


def _softplus(x):
    return jnp.maximum(x, 0.0) + jnp.log1p(jnp.exp(-jnp.abs(x)))


def _one_minus_exp(y):
    p = jnp.full_like(y, 1.0 / 5040.0)
    for c in (1.0 / 720.0, 1.0 / 120.0, 1.0 / 24.0, 1.0 / 6.0, 0.5, 1.0):
        p = p * y + c
    return jnp.where(y > -0.25, -y * p, 1.0 - jnp.exp(y))


def _modulated_norm(x, mod, g):
    ms = jnp.mean(x * x, axis=-1, keepdims=True)
    h = x * lax.rsqrt(ms + EPS) * g
    return h * (1.0 + mod[1:2]) + mod[0:1]


def _gated_residual(x, y, mod, g):
    ms = jnp.mean(y * y, axis=-1, keepdims=True)
    return x + mod[2:3] * (y * lax.rsqrt(ms + EPS) * g)


def _adaln_kernel(c_ref, w_ref, b_ref, o_ref):
    sc = _silu(c_ref[...]).astype(BF16)
    o_ref[0] = jnp.dot(sc, w_ref[0].astype(BF16), preferred_element_type=F32) + b_ref[0]


def _adaln(c_all, ada_w, ada_b):
    depth, d, d3 = ada_w.shape
    n = c_all.shape[0]
    nj = d3 // d
    return pl.pallas_call(
        _adaln_kernel,
        grid=(depth, nj),
        in_specs=[
            pl.BlockSpec((n, d), lambda i, j: (0, 0)),
            pl.BlockSpec((1, d, d), lambda i, j: (i, 0, j)),
            pl.BlockSpec((1, 1, d), lambda i, j: (i, 0, j)),
        ],
        out_specs=pl.BlockSpec((1, n, d), lambda i, j: (i, 0, j)),
        out_shape=jax.ShapeDtypeStruct((depth, n, d3), F32),
        compiler_params=_params(2),
        name="adaln",
    )(c_all, ada_w, ada_b.reshape(depth, 1, d3))


def _lru_kernel(x_ref, mod_ref, gpre_ref, gpost_ref, win_ref, cw_ref, cb_ref, wax_ref,
                ba_ref, bx_ref, lam_ref, wout_ref, conv0_ref, h0_ref,
                y_ref, convo_ref, hlast_ref,
                xbuf, xc_s, pre_a, pre_x, gate_s, yv_s, hcar, *, tt, r, nblk):
    t = pl.program_id(1)
    nt = pl.num_programs(1)
    tail0 = SUBLANES - (CONV_TAPS - 1)

    @pl.when(t == 0)
    def _():
        xbuf[tail0:SUBLANES, :] = conv0_ref[0]
        hcar[...] = h0_ref[0]

    x = x_ref[0]
    mod = mod_ref[0]
    h = _modulated_norm(x, mod, gpre_ref[...])
    z = jnp.dot(h.astype(BF16), win_ref[...], preferred_element_type=F32)
    xbuf[SUBLANES:SUBLANES + tt, :] = z[:, :r]
    gate_s[...] = z[:, r:]

    cw = cw_ref[...]
    xc = cb_ref[...] + xbuf[tail0:tail0 + tt, :] * cw[0:1]
    for k in range(1, CONV_TAPS):
        xc = xc + xbuf[tail0 + k:tail0 + k + tt, :] * cw[k:k + 1]
    tail = xbuf[tt + tail0:tt + SUBLANES, :]
    xbuf[tail0:SUBLANES, :] = tail
    xc_s[...] = xc

    @pl.when(t == nt - 1)
    def _():
        convo_ref[0] = tail

    xcb = xc.astype(BF16)
    for n in range(nblk):
        lo, hi = n * LANES, (n + 1) * LANES
        res = jnp.dot(xcb[:, lo:hi], wax_ref[n], preferred_element_type=F32)
        pre_a[:, lo:hi] = res[:, :LANES]
        pre_x[:, lo:hi] = res[:, LANES:]

    coef = -LRU_C * _softplus(-lam_ref[...])
    ba = ba_ref[...]
    bx = bx_ref[...]
    row = lax.broadcasted_iota(jnp.int32, (SUBLANES, r), 0)

    def group(g, hprev):
        st = pl.multiple_of(g * SUBLANES, SUBLANES)
        sl = pl.ds(st, SUBLANES)
        rg = jax.nn.sigmoid(pre_a[sl, :] + ba)
        ig = jax.nn.sigmoid(pre_x[sl, :] + bx)
        log_a = coef * rg
        a = jnp.exp(log_a)
        u = jnp.sqrt(_one_minus_exp(2.0 * log_a)) * (ig * xc_s[sl, :])
        for s in (1, 2, 4):
            a_sh = jnp.where(row >= s, pltpu.roll(a, s, axis=0), 1.0)
            u_sh = jnp.where(row >= s, pltpu.roll(u, s, axis=0), 0.0)
            u = a * u_sh + u
            a = a * a_sh
        hs = a * hprev + u
        yv_s[sl, :] = hs * _silu(gate_s[sl, :])
        return hs[SUBLANES - 1:SUBLANES, :]

    hlast = lax.fori_loop(0, tt // SUBLANES, group, hcar[...])
    hcar[...] = hlast

    @pl.when(t == nt - 1)
    def _():
        hlast_ref[0] = hlast

    y = jnp.dot(yv_s[...].astype(BF16), wout_ref[...], preferred_element_type=F32)
    y_ref[0] = _gated_residual(x, y, mod, gpost_ref[...])


def _lru_layer(x, mod, mod_off, gpre, gpost, w_in, conv_w, conv_b, w_ax, b_a, b_x, lam, w_out,
               conv0, h0, *, tt):
    b, t, d = x.shape
    r = lam.shape[-1]
    nblk = w_ax.shape[0]
    kern = functools.partial(_lru_kernel, tt=tt, r=r, nblk=nblk)
    return pl.pallas_call(
        kern,
        grid=(b, t // tt),
        in_specs=[
            pl.BlockSpec((1, tt, d), lambda i, j: (i, j, 0)),
            pl.BlockSpec((1, 3, d), lambda i, j: (i + mod_off, 0, 0)),
            _const_spec((1, d)), _const_spec((1, d)),
            _const_spec((d, 2 * r)),
            _const_spec((CONV_TAPS, r)), _const_spec((1, r)),
            _const_spec((nblk, LANES, 2 * LANES)),
            _const_spec((1, r)), _const_spec((1, r)), _const_spec((1, r)),
            _const_spec((r, d)),
            pl.BlockSpec((1, CONV_TAPS - 1, r), lambda i, j: (i, 0, 0)),
            pl.BlockSpec((1, 1, r), lambda i, j: (i, 0, 0)),
        ],
        out_specs=[
            pl.BlockSpec((1, tt, d), lambda i, j: (i, j, 0)),
            pl.BlockSpec((1, CONV_TAPS - 1, r), lambda i, j: (i, 0, 0)),
            pl.BlockSpec((1, 1, r), lambda i, j: (i, 0, 0)),
        ],
        out_shape=[
            jax.ShapeDtypeStruct((b, t, d), F32),
            jax.ShapeDtypeStruct((b, CONV_TAPS - 1, r), F32),
            jax.ShapeDtypeStruct((b, 1, r), F32),
        ],
        scratch_shapes=[
            pltpu.VMEM((tt + SUBLANES, r), F32),
            pltpu.VMEM((tt, r), F32), pltpu.VMEM((tt, r), F32), pltpu.VMEM((tt, r), F32),
            pltpu.VMEM((tt, r), F32), pltpu.VMEM((tt, r), F32),
            pltpu.VMEM((1, r), F32),
        ],
        compiler_params=_params(2),
        name="lru_layer",
    )(x, mod, gpre, gpost, w_in, conv_w, conv_b, w_ax, b_a, b_x, lam, w_out, conv0, h0)


def _split3(x):
    x1 = x.astype(BF16)
    r1 = x - x1.astype(F32)
    x2 = r1.astype(BF16)
    x3 = (r1 - x2.astype(F32)).astype(BF16)
    return x1, x2, x3


def _tri_cumsum(tri, x, carry):
    c = carry
    for piece in _split3(x):
        c = c + jnp.dot(tri, piece, preferred_element_type=F32)
    return c


def _fox_in_kernel(x_ref, mod_ref, gpre_ref, w_ref, wf_ref, bf_ref, tri_ref,
                   q_ref, kb_ref, k_ref, v_ref, sg_ref, lf_ref, cum_ref, aug_ref, car,
                   *, w, nh, qscale):
    @pl.when(pl.program_id(1) == 0)
    def _():
        car[...] = jnp.zeros_like(car)

    hb = _modulated_norm(x_ref[0], mod_ref[0], gpre_ref[...]).astype(BF16)
    z = jnp.dot(hb, w_ref[...], preferred_element_type=F32)
    q_ref[0] = (z[:, :w] * qscale).astype(BF16)
    k = z[:, w:2 * w]
    k_ref[0] = k
    kb_ref[0] = k.astype(BF16)
    v_ref[0] = z[:, 2 * w:3 * w]
    sg_ref[0] = _silu(z[:, 3 * w:]).astype(BF16)
    fl = jnp.dot(hb, wf_ref[...], preferred_element_type=F32) + bf_ref[...]
    lf = -_softplus(-fl)
    c = _tri_cumsum(tri_ref[...], lf, car[...])
    tt = c.shape[0]
    car[...] = c[tt - 1:tt, :]
    lf_ref[0] = lf[:, :nh]
    cum_ref[0] = c[:, :nh]
    y1, y2, y3 = _split3(c * (-LOG2E))
    lane = lax.broadcasted_iota(jnp.int32, c.shape, 1)
    aug = jnp.where(lane < nh, y1.astype(F32),
                    jnp.where(lane < 2 * nh, y2.astype(F32),
                              jnp.where(lane < 3 * nh, y3.astype(F32), 0.0)))
    aug_ref[0] = aug.astype(BF16)


def _tri(n):
    return jnp.tril(jnp.ones((n, n), F32)).astype(BF16)


def _fox_in(x, mod, mod_off, gpre, w_main, w_f3, b_f3, nh, *, tt, qscale):
    b, t, d = x.shape
    w = w_main.shape[1] // 4
    kern = functools.partial(_fox_in_kernel, w=w, nh=nh, qscale=qscale)
    tile = lambda dt: jax.ShapeDtypeStruct((b, t, w), dt)
    tile_spec = pl.BlockSpec((1, tt, w), lambda i, j: (i, j, 0))
    head_spec = pl.BlockSpec((1, tt, nh), lambda i, j: (i, j, 0))
    return pl.pallas_call(
        kern,
        grid=(b, t // tt),
        in_specs=[
            pl.BlockSpec((1, tt, d), lambda i, j: (i, j, 0)),
            pl.BlockSpec((1, 3, d), lambda i, j: (i + mod_off, 0, 0)),
            _const_spec((1, d)),
            _const_spec((d, 4 * w)),
            _const_spec((d, LANES)),
            _const_spec((1, LANES)),
            _const_spec((tt, tt)),
        ],
        out_specs=[tile_spec] * 5 + [head_spec, head_spec,
                                     pl.BlockSpec((1, tt, LANES), lambda i, j: (i, j, 0))],
        out_shape=[tile(BF16), tile(BF16), tile(F32), tile(F32), tile(BF16),
                   jax.ShapeDtypeStruct((b, t, nh), F32), jax.ShapeDtypeStruct((b, t, nh), F32),
                   jax.ShapeDtypeStruct((b, t, LANES), BF16)],
        scratch_shapes=[pltpu.VMEM((1, LANES), F32)],
        compiler_params=_params(2),
        name="fox_in",
    )(x, mod, gpre, w_main, w_f3, b_f3, _tri(tt))


def _cumsum_kernel(x_ref, tri_ref, o_ref, car):
    @pl.when(pl.program_id(1) == 0)
    def _():
        car[...] = jnp.zeros_like(car)

    c = _tri_cumsum(tri_ref[...], x_ref[0], car[...])
    o_ref[0] = c
    tc = c.shape[0]
    car[...] = c[tc - 1:tc, :]


def _cumsum_time(x, *, tc):
    b, t, nh = x.shape
    return pl.pallas_call(
        _cumsum_kernel,
        grid=(b, t // tc),
        in_specs=[pl.BlockSpec((1, tc, nh), lambda i, j: (i, j, 0)), _const_spec((tc, tc))],
        out_specs=pl.BlockSpec((1, tc, nh), lambda i, j: (i, j, 0)),
        out_shape=jax.ShapeDtypeStruct((b, t, nh), F32),
        scratch_shapes=[pltpu.VMEM((1, nh), F32)],
        compiler_params=_params(2),
        name="cumsum_time",
    )(x, _tri(tc))


def _split_heads(q):
    qf = q.astype(F32)
    lane = lax.broadcasted_iota(jnp.int32, qf.shape, 1)
    return (jnp.where(lane < HEAD_DIM, qf, 0.0).astype(BF16),
            jnp.where(lane >= HEAD_DIM, qf, 0.0).astype(BF16))


def _attn_kernel(q_ref, k_ref, a_ref, vt_ref, o_ref, z0, z1, zm0, zm1, p0, p1, al0, al1, m_s, acc_s,
                 *, tq, nh):
    hp = pl.program_id(1)
    qi = pl.program_id(2)
    z_s, zmax_s, p_s, alpha_s = (z0, z1), (zm0, zm1), (p0, p1), (al0, al1)

    @pl.when((pl.program_id(0) == 0) & (hp == 0) & (qi == 0))
    def _():
        p1[...] = jnp.zeros_like(p1)
        al1[...] = jnp.ones_like(al1)

    q = q_ref[0].astype(F32)
    lane = lax.broadcasted_iota(jnp.int32, (tq, LANES), 1)
    qa = []
    for hh in range(2):
        head = 2 * hp + hh
        own = (lane >= hh * HEAD_DIM) & (lane < (hh + 1) * HEAD_DIM)
        pick = (lane == head) | (lane == head + nh) | (lane == head + 2 * nh)
        qa.append(jnp.concatenate(
            [jnp.where(own, q, 0.0).astype(BF16), jnp.where(pick, 1.0, 0.0).astype(BF16)], axis=1))
    m_s[...] = jnp.full_like(m_s, NEG_BIG)
    acc_s[...] = jnp.zeros_like(acc_s)

    def qk(blk, slot, masked):
        st = pl.multiple_of(blk * tq, tq)
        keys = jnp.concatenate([k_ref[0, pl.ds(st, tq), :], a_ref[0, pl.ds(st, tq), :]], axis=1)
        for hh in range(2):
            z = lax.dot_general(keys, qa[hh], _NT, preferred_element_type=F32)
            if masked:
                kpos = lax.broadcasted_iota(jnp.int32, z.shape, 0)
                qpos = lax.broadcasted_iota(jnp.int32, z.shape, 1)
                z = jnp.where(kpos <= qpos, z, NEG_BIG)
            z_s[slot][hh] = z
            zmax_s[slot][hh] = jnp.max(z, axis=0, keepdims=True)

    def softmax(slot):
        for hh in range(2):
            m_prev = m_s[hh]
            m_next = jnp.maximum(m_prev, zmax_s[slot][hh])
            p_s[slot][hh] = jnp.exp2(z_s[slot][hh] - m_next).astype(BF16)
            alpha_s[slot][hh] = jnp.exp2(m_prev - m_next)
            m_s[hh] = m_next

    def pv(blk, slot, valid):
        for hh in range(2):
            r = jnp.dot(vt_ref[0, hh, blk], p_s[slot][hh], preferred_element_type=F32)
            if valid is not None:
                r = jnp.where(valid, r, 0.0)
            acc_s[hh] = acc_s[hh] * alpha_s[slot][hh] + r

    def iteration(i, slot):
        pv(jnp.minimum(qi - i + 1, qi), 1 - slot, i >= 1)
        softmax(slot)
        qk(qi - i - 1, 1 - slot, False)

    def finish(last):
        pv(jnp.minimum(1, qi), 1 - last, qi >= 1)
        softmax(last)
        pv(0, last, None)
        outs = []
        for hh in range(2):
            acc = acc_s[hh]
            outs.append(acc[:HEAD_DIM] / acc[HEAD_DIM:HEAD_DIM + 1])
        o_ref[0] = jnp.concatenate(outs, axis=0).T.astype(o_ref.dtype)

    qk(qi, 0, True)

    def pair(k, carry):
        iteration(2 * k, 0)
        iteration(2 * k + 1, 1)
        return carry

    lax.fori_loop(0, qi // 2, pair, 0)
    odd = lax.rem(qi, 2) == 1

    @pl.when(odd)
    def _():
        iteration(qi - 1, 0)
        finish(1)

    @pl.when(jnp.logical_not(odd))
    def _():
        finish(0)


def _attention(qb, kb, aug, vt, nh, *, tq):
    b, t, w = qb.shape
    hp = w // LANES
    nq = t // tq
    rows = vt.shape[3]
    kern = functools.partial(_attn_kernel, tq=tq, nh=nh)
    return pl.pallas_call(
        kern,
        grid=(b, hp, nq),
        in_specs=[
            pl.BlockSpec((1, tq, LANES), lambda i, h, j: (i, j, h)),
            pl.BlockSpec((1, t, LANES), lambda i, h, j: (i, 0, h)),
            pl.BlockSpec((1, t, LANES), lambda i, h, j: (i, 0, 0)),
            pl.BlockSpec((1, 2, nq, rows, tq), lambda i, h, j: (i, h, 0, 0, 0)),
        ],
        out_specs=pl.BlockSpec((1, tq, LANES), lambda i, h, j: (i, j, h)),
        out_shape=jax.ShapeDtypeStruct((b, t, w), BF16),
        scratch_shapes=(
            [pltpu.VMEM((2, tq, tq), F32)] * 2
            + [pltpu.VMEM((2, 1, tq), F32)] * 2
            + [pltpu.VMEM((2, tq, tq), BF16)] * 2
            + [pltpu.VMEM((2, 1, tq), F32)] * 2
            + [pltpu.VMEM((2, 1, tq), F32),
               pltpu.VMEM((2, rows, tq), F32)]),
        compiler_params=_params(3),
        name="fox_attention",
    )(qb, kb, aug, vt)


def _values_transposed(v, *, tk):
    b, t, w = v.shape
    nh = w // HEAD_DIM
    nk = t // tk
    vt = v.astype(BF16).reshape(b, nk, tk, nh, HEAD_DIM).transpose(0, 3, 1, 4, 2)
    return jnp.concatenate([vt, jnp.ones((b, nh, nk, ONES_ROWS, tk), BF16)], axis=3)


def _attn_cached_kernel(q_ref, kn_ref, vn_ref, pk_ref, pv_ref, cp_ref, cn_ref, o_ref):
    qh = _split_heads(q_ref[0])
    pkb = pk_ref[0].astype(BF16)
    pvb = pv_ref[0].astype(BF16)
    kn = kn_ref[0]
    vn = vn_ref[0]
    t = kn.shape[0]
    p_len = pkb.shape[0]
    rowi = lax.broadcasted_iota(jnp.int32, (t, t), 0)
    coli = lax.broadcasted_iota(jnp.int32, (t, t), 1)
    outs = []
    for hh in range(2):
        cp = cp_ref[0, 0, hh:hh + 1, :]
        tot = cp[:, p_len - 1:p_len]
        z1 = lax.dot_general(qh[hh], pkb, _NT, preferred_element_type=F32) - (cp - tot) * LOG2E
        z2 = (lax.dot_general(qh[hh], kn, _NT, preferred_element_type=F32)
              - cn_ref[0, 0, hh:hh + 1, :] * LOG2E)
        z2 = jnp.where(coli <= rowi, z2, NEG_BIG)
        m = jnp.maximum(jnp.max(z1, axis=1), jnp.max(z2, axis=1))[:, None]
        p1 = jnp.exp2(z1 - m)
        p2 = jnp.exp2(z2 - m)
        l = (jnp.sum(p1, axis=1) + jnp.sum(p2, axis=1))[:, None]
        o = (jnp.dot(p1.astype(BF16), pvb, preferred_element_type=F32)
             + jnp.dot(p2.astype(BF16), vn, preferred_element_type=F32))
        outs.append(o / l)
    lane = lax.broadcasted_iota(jnp.int32, (t, LANES), 1)
    o_ref[0] = jnp.where(lane < HEAD_DIM, outs[0], outs[1]).astype(o_ref.dtype)


def _attention_cached(qb, kb, vb, past_k, past_v, cum_past, cum_new):
    b, t, w = qb.shape
    p_len = past_k.shape[1]
    hp = w // LANES
    new_spec = pl.BlockSpec((1, t, LANES), lambda i, h: (i, 0, h))
    past_spec = pl.BlockSpec((1, p_len, LANES), lambda i, h: (i, 0, h))
    return pl.pallas_call(
        _attn_cached_kernel,
        grid=(b, hp),
        in_specs=[new_spec, new_spec, new_spec, past_spec, past_spec,
                  pl.BlockSpec((1, 1, 2, p_len), lambda i, h: (i, h, 0, 0)),
                  pl.BlockSpec((1, 1, 2, t), lambda i, h: (i, h, 0, 0))],
        out_specs=new_spec,
        out_shape=jax.ShapeDtypeStruct((b, t, w), BF16),
        compiler_params=_params(2),
        name="fox_attention_cached",
    )(qb, kb, vb, past_k, past_v,
      cum_past.reshape(b, hp, 2, p_len), cum_new.reshape(b, hp, 2, t))


def _fox_out_kernel(o_ref, sg_ref, x_ref, mod_ref, gpost_ref, w_ref, y_ref):
    yv = (o_ref[0].astype(F32) * sg_ref[0].astype(F32)).astype(BF16)
    y = jnp.dot(yv, w_ref[...], preferred_element_type=F32)
    y_ref[0] = _gated_residual(x_ref[0], y, mod_ref[0], gpost_ref[...])


def _fox_out(o, g, x, mod, mod_off, gpost, w_out, *, tt):
    b, t, d = x.shape
    w = o.shape[-1]
    return pl.pallas_call(
        _fox_out_kernel,
        grid=(b, t // tt),
        in_specs=[
            pl.BlockSpec((1, tt, w), lambda i, j: (i, j, 0)),
            pl.BlockSpec((1, tt, w), lambda i, j: (i, j, 0)),
            pl.BlockSpec((1, tt, d), lambda i, j: (i, j, 0)),
            pl.BlockSpec((1, 3, d), lambda i, j: (i + mod_off, 0, 0)),
            _const_spec((1, d)),
            _const_spec((w, d)),
        ],
        out_specs=pl.BlockSpec((1, tt, d), lambda i, j: (i, j, 0)),
        out_shape=jax.ShapeDtypeStruct((b, t, d), F32),
        compiler_params=_params(2),
        name="fox_out",
    )(o, g, x, mod, gpost, w_out)


def _tile(t, pref):
    return pref if t % pref == 0 else t


def kernel(x_prompt, x_sample, c_prompt, c_sample, state_lru_h, state_lru_conv, cache_fox_k, cache_fox_v, cache_fox_logf, norm_pre, norm_post, ada_w, ada_b, lru_w_in, lru_conv_w, lru_conv_b, lru_w_a, lru_b_a, lru_w_x, lru_b_x, lru_lambda, lru_w_out, fox_w_in, fox_b_f, fox_w_out):
    b, t, d = x_prompt.shape
    db, dt, _ = x_sample.shape
    depth = ada_w.shape[0]
    r = lru_lambda.shape[-1]
    nh = fox_b_f.shape[-1]
    w = fox_w_out.shape[1]
    p_len = cache_fox_k.shape[2]
    scale = float(HEAD_DIM) ** -0.5

    c_all = jnp.concatenate([c_prompt, c_sample], axis=0)
    mod = _adaln(c_all, ada_w, ada_b).reshape(depth, b + db, 3, d)

    xp, xs = x_prompt, x_sample

    j = 0
    w_ax = jnp.concatenate([lru_w_a[j], lru_w_x[j]], axis=-1).astype(BF16)
    lru_args = (norm_pre[0:1], norm_post[0:1], lru_w_in[j].astype(BF16), lru_conv_w[j],
                lru_conv_b[j:j + 1], w_ax, lru_b_a[j:j + 1], lru_b_x[j:j + 1],
                lru_lambda[j:j + 1], lru_w_out[j].astype(BF16))
    xp, conv_p, h_p = _lru_layer(
        xp, mod[0], 0, *lru_args,
        jnp.zeros((b, CONV_TAPS - 1, r), F32), jnp.zeros((b, 1, r), F32), tt=_tile(t, 256))
    xs, conv_s, h_s = _lru_layer(
        xs, mod[0], b, *lru_args,
        state_lru_conv[j], state_lru_h[j].reshape(db, 1, r), tt=_tile(dt, 256))

    w_main = fox_w_in[j][:, :4 * w].astype(BF16)
    pad = jnp.zeros((d, LANES - 3 * nh), F32)
    w_f3 = jnp.concatenate([fox_w_in[j][:, 4 * w:]] * 3 + [pad], axis=1).astype(BF16)
    b_f3 = jnp.concatenate([fox_b_f[j]] * 3 + [jnp.zeros((LANES - 3 * nh,), F32)]).reshape(1, LANES)
    w_out = fox_w_out[j].astype(BF16)
    qscale = scale * LOG2E

    qb, kb, k_p, v_p, sg_p, lf_p, _, aug_p = _fox_in(
        xp, mod[1], 0, norm_pre[1:2], w_main, w_f3, b_f3, nh, tt=_tile(t, 256), qscale=qscale)
    tq = _tile(t, 512)
    o_p = _attention(qb, kb, aug_p, _values_transposed(v_p, tk=tq), nh, tq=tq)
    yp = _fox_out(o_p, sg_p, xp, mod[1], 0, norm_post[1:2], w_out, tt=_tile(t, 512))

    qs, ksb, k_s, v_s, sg_s, lf_s, cum_s, _ = _fox_in(
        xs, mod[1], b, norm_pre[1:2], w_main, w_f3, b_f3, nh, tt=_tile(dt, 256), qscale=qscale)
    cum_past = _cumsum_time(cache_fox_logf[j], tc=_tile(p_len, 512))
    o_s = _attention_cached(qs, ksb, v_s.astype(BF16), cache_fox_k[j].reshape(db, p_len, w),
                            cache_fox_v[j].reshape(db, p_len, w),
                            jnp.swapaxes(cum_past, 1, 2), jnp.swapaxes(cum_s, 1, 2))
    ys = _fox_out(o_s, sg_s, xs, mod[1], b, norm_post[1:2], w_out, tt=_tile(dt, 512))

    hd = (nh, HEAD_DIM)
    return (yp, ys,
            h_p.reshape(1, b, r), conv_p[None],
            k_p.reshape(1, b, t, *hd), v_p.reshape(1, b, t, *hd), lf_p[None],
            h_s.reshape(1, db, r), conv_s[None],
            k_s.reshape(1, db, dt, *hd), v_s.reshape(1, db, dt, *hd), lf_s[None])
```

```python
import functools

import jax
import jax.numpy as jnp
from jax import lax
from jax.experimental import pallas as pl
from jax.experimental.pallas import tpu as pltpu

F32 = jnp.float32
BF16 = jnp.bfloat16

EPS = 1e-6
LRU_C = 8.0
CONV_TAPS = 4
HEAD_DIM = 64
LANES = 128
SUBLANES = 8
NEG_BIG = -1e30
LOG2E = 1.4426950408889634
ONES_ROWS = 16
VMEM_LIMIT_BYTES = 56 * 1024 * 1024

_NT = (((1,), (1,)), ((), ()))


def _params(n_grid):
    return pltpu.CompilerParams(
        dimension_semantics=("arbitrary",) * n_grid,
        vmem_limit_bytes=VMEM_LIMIT_BYTES)


def _const_spec(shape):
    nd = len(shape)
    return pl.BlockSpec(shape, lambda *_: (0,) * nd, pipeline_mode=pl.Buffered(1))


def _sigmoid(x):
    return 0.5 * jnp.tanh(0.5 * x) + 0.5


def _silu(x):
    return x * _sigmoid(x)


def _softplus(x):
    return jnp.maximum(x, 0.0) + jnp.log1p(jnp.exp(-jnp.abs(x)))


def _one_minus_exp_neg(t, exp_neg_t):
    p = jnp.full_like(t, -1.0 / 24.0)
    for c in (1.0 / 6.0, -0.5, 1.0):
        p = p * t + c
    return jnp.where(t < 1.0 / 64.0, t * p, 1.0 - exp_neg_t)


def _sqrt_pos(x):
    x = jnp.maximum(x, 1e-30)
    return x * lax.rsqrt(x)


def _modulated_norm(x, mod, g):
    ms = jnp.mean(x * x, axis=-1, keepdims=True)
    h = x * lax.rsqrt(ms + EPS) * g
    return h * (1.0 + mod[1:2]) + mod[0:1]


def _gated_residual(x, y, mod, g):
    ms = jnp.mean(y * y, axis=-1, keepdims=True)
    return x + mod[2:3] * (y * lax.rsqrt(ms + EPS) * g)


def _adaln_kernel(c_ref, w_ref, b_ref, o_ref):
    sc = _silu(c_ref[...]).astype(BF16)
    o_ref[0] = jnp.dot(sc, w_ref[0].astype(BF16), preferred_element_type=F32) + b_ref[0]


def _adaln(c_all, ada_w, ada_b):
    depth, d, d3 = ada_w.shape
    n = c_all.shape[0]
    nj = d3 // d
    return pl.pallas_call(
        _adaln_kernel,
        grid=(depth, nj),
        in_specs=[
            pl.BlockSpec((n, d), lambda i, j: (0, 0)),
            pl.BlockSpec((1, d, d), lambda i, j: (i, 0, j)),
            pl.BlockSpec((1, 1, d), lambda i, j: (i, 0, j)),
        ],
        out_specs=pl.BlockSpec((1, n, d), lambda i, j: (i, 0, j)),
        out_shape=jax.ShapeDtypeStruct((depth, n, d3), F32),
        compiler_params=_params(2),
        name="adaln",
    )(c_all, ada_w, ada_b.reshape(depth, 1, d3))


def _lru_kernel(x_ref, mod_ref, gpre_ref, gpost_ref, win_ref, cw_ref, cb_ref, wax_ref,
                ba_ref, bx_ref, lam_ref, wout_ref, conv0_ref, h0_ref,
                y_ref, convo_ref, hlast_ref,
                xbuf, xc_s, pre_a, pre_x, gate_s, yv_s, hcar, *, tt, r, nblk):
    t = pl.program_id(1)
    nt = pl.num_programs(1)
    tail0 = SUBLANES - (CONV_TAPS - 1)

    @pl.when(t == 0)
    def _():
        xbuf[tail0:SUBLANES, :] = conv0_ref[0]
        hcar[...] = h0_ref[0]

    x = x_ref[0]
    mod = mod_ref[0]
    h = _modulated_norm(x, mod, gpre_ref[...])
    z = jnp.dot(h.astype(BF16), win_ref[...], preferred_element_type=F32)
    xbuf[SUBLANES:SUBLANES + tt, :] = z[:, :r]
    gate_s[...] = z[:, r:]

    cw = cw_ref[...]
    xc = cb_ref[...] + xbuf[tail0:tail0 + tt, :] * cw[0:1]
    for k in range(1, CONV_TAPS):
        xc = xc + xbuf[tail0 + k:tail0 + k + tt, :] * cw[k:k + 1]
    tail = xbuf[tt + tail0:tt + SUBLANES, :]
    xbuf[tail0:SUBLANES, :] = tail
    xc_s[...] = xc

    @pl.when(t == nt - 1)
    def _():
        convo_ref[0] = tail

    xcb = xc.astype(BF16)
    for n in range(nblk):
        lo, hi = n * LANES, (n + 1) * LANES
        res = jnp.dot(xcb[:, lo:hi], wax_ref[n], preferred_element_type=F32)
        pre_a[:, lo:hi] = res[:, :LANES]
        pre_x[:, lo:hi] = res[:, LANES:]

    coef = -LRU_C * _softplus(-lam_ref[...])
    coef2n = -2.0 * coef
    ba = ba_ref[...]
    bx = bx_ref[...]
    row = lax.broadcasted_iota(jnp.int32, (SUBLANES, r), 0)

    def group(g, hprev):
        st = pl.multiple_of(g * SUBLANES, SUBLANES)
        sl = pl.ds(st, SUBLANES)
        rg = _sigmoid(pre_a[sl, :] + ba)
        ig = _sigmoid(pre_x[sl, :] + bx)
        a = jnp.exp(coef * rg)
        u = _sqrt_pos(_one_minus_exp_neg(coef2n * rg, a * a)) * (ig * xc_s[sl, :])
        for s in (1, 2, 4):
            a_sh = jnp.where(row >= s, pltpu.roll(a, s, axis=0), 1.0)
            u_sh = jnp.where(row >= s, pltpu.roll(u, s, axis=0), 0.0)
            u = a * u_sh + u
            a = a * a_sh
        hs = a * hprev + u
        yv_s[sl, :] = hs * _silu(gate_s[sl, :])
        return hs[SUBLANES - 1:SUBLANES, :]

    hlast = lax.fori_loop(0, tt // SUBLANES, group, hcar[...])
    hcar[...] = hlast

    @pl.when(t == nt - 1)
    def _():
        hlast_ref[0] = hlast

    y = jnp.dot(yv_s[...].astype(BF16), wout_ref[...], preferred_element_type=F32)
    y_ref[0] = _gated_residual(x, y, mod, gpost_ref[...])


def _lru_layer(x, mod, mod_off, gpre, gpost, w_in, conv_w, conv_b, w_ax, b_a, b_x, lam, w_out,
               conv0, h0, *, tt):
    b, t, d = x.shape
    r = lam.shape[-1]
    nblk = w_ax.shape[0]
    kern = functools.partial(_lru_kernel, tt=tt, r=r, nblk=nblk)
    return pl.pallas_call(
        kern,
        grid=(b, t // tt),
        in_specs=[
            pl.BlockSpec((1, tt, d), lambda i, j: (i, j, 0)),
            pl.BlockSpec((1, 3, d), lambda i, j: (i + mod_off, 0, 0)),
            _const_spec((1, d)), _const_spec((1, d)),
            _const_spec((d, 2 * r)),
            _const_spec((CONV_TAPS, r)), _const_spec((1, r)),
            _const_spec((nblk, LANES, 2 * LANES)),
            _const_spec((1, r)), _const_spec((1, r)), _const_spec((1, r)),
            _const_spec((r, d)),
            pl.BlockSpec((1, CONV_TAPS - 1, r), lambda i, j: (i, 0, 0)),
            pl.BlockSpec((1, 1, r), lambda i, j: (i, 0, 0)),
        ],
        out_specs=[
            pl.BlockSpec((1, tt, d), lambda i, j: (i, j, 0)),
            pl.BlockSpec((1, CONV_TAPS - 1, r), lambda i, j: (i, 0, 0)),
            pl.BlockSpec((1, 1, r), lambda i, j: (i, 0, 0)),
        ],
        out_shape=[
            jax.ShapeDtypeStruct((b, t, d), F32),
            jax.ShapeDtypeStruct((b, CONV_TAPS - 1, r), F32),
            jax.ShapeDtypeStruct((b, 1, r), F32),
        ],
        scratch_shapes=[
            pltpu.VMEM((tt + SUBLANES, r), F32),
            pltpu.VMEM((tt, r), F32), pltpu.VMEM((tt, r), F32), pltpu.VMEM((tt, r), F32),
            pltpu.VMEM((tt, r), F32), pltpu.VMEM((tt, r), F32),
            pltpu.VMEM((1, r), F32),
        ],
        compiler_params=_params(2),
        name="lru_layer",
    )(x, mod, gpre, gpost, w_in, conv_w, conv_b, w_ax, b_a, b_x, lam, w_out, conv0, h0)


def _split3(x):
    x1 = x.astype(BF16)
    r1 = x - x1.astype(F32)
    x2 = r1.astype(BF16)
    x3 = (r1 - x2.astype(F32)).astype(BF16)
    return x1, x2, x3


def _tri_cumsum(tri, x, carry):
    c = carry
    for piece in _split3(x):
        c = c + jnp.dot(tri, piece, preferred_element_type=F32)
    return c


def _fox_in_kernel(x_ref, mod_ref, gpre_ref, w_ref, wvt_ref, wf_ref, bf_ref, tri_ref,
                   q_ref, kb_ref, k_ref, v_ref, sg_ref, vt_ref, lf_ref, cum_ref, aug_ref, car,
                   *, w, nh, qscale):
    @pl.when(pl.program_id(1) == 0)
    def _():
        car[...] = jnp.zeros_like(car)

    hb = _modulated_norm(x_ref[0], mod_ref[0], gpre_ref[...]).astype(BF16)
    z = jnp.dot(hb, w_ref[...], preferred_element_type=F32)
    q_ref[0] = (z[:, :w] * qscale).astype(BF16)
    k = z[:, w:2 * w]
    k_ref[0] = k
    kb_ref[0] = k.astype(BF16)
    v_ref[0] = z[:, 2 * w:3 * w]
    sg_ref[0] = _silu(z[:, 3 * w:]).astype(BF16)
    vt = lax.dot_general(wvt_ref[...], hb, _NT, preferred_element_type=F32)
    tt = vt.shape[1]
    vt = vt.reshape(nh, HEAD_DIM, tt)
    vt_ref[0, :, 0] = jnp.concatenate([vt, jnp.ones((nh, ONES_ROWS, tt), F32)], axis=1).astype(BF16)
    fl = jnp.dot(hb, wf_ref[...], preferred_element_type=F32) + bf_ref[...]
    lf = -_softplus(-fl)
    c = _tri_cumsum(tri_ref[...], lf, car[...])
    tt = c.shape[0]
    car[...] = c[tt - 1:tt, :]
    lf_ref[0] = lf[:, :nh]
    cum_ref[0] = c[:, :nh]
    y1, y2, y3 = _split3(c * (-LOG2E))
    lane = lax.broadcasted_iota(jnp.int32, c.shape, 1)
    aug = jnp.where(lane < nh, y1.astype(F32),
                    jnp.where(lane < 2 * nh, y2.astype(F32),
                              jnp.where(lane < 3 * nh, y3.astype(F32), 0.0)))
    aug_ref[0] = aug.astype(BF16)


def _tri(n):
    return jnp.tril(jnp.ones((n, n), F32)).astype(BF16)


def _fox_in(x, mod, mod_off, gpre, w_main, w_vt, w_f3, b_f3, nh, *, tt, qscale):
    b, t, d = x.shape
    w = w_main.shape[1] // 4
    rows = HEAD_DIM + ONES_ROWS
    kern = functools.partial(_fox_in_kernel, w=w, nh=nh, qscale=qscale)
    tile = lambda dt: jax.ShapeDtypeStruct((b, t, w), dt)
    tile_spec = pl.BlockSpec((1, tt, w), lambda i, j: (i, j, 0))
    head_spec = pl.BlockSpec((1, tt, nh), lambda i, j: (i, j, 0))
    return pl.pallas_call(
        kern,
        grid=(b, t // tt),
        in_specs=[
            pl.BlockSpec((1, tt, d), lambda i, j: (i, j, 0)),
            pl.BlockSpec((1, 3, d), lambda i, j: (i + mod_off, 0, 0)),
            _const_spec((1, d)),
            _const_spec((d, 4 * w)),
            _const_spec((w, d)),
            _const_spec((d, LANES)),
            _const_spec((1, LANES)),
            _const_spec((tt, tt)),
        ],
        out_specs=[tile_spec] * 5 + [pl.BlockSpec((1, nh, 1, rows, tt), lambda i, j: (i, 0, j, 0, 0)),
                                     head_spec, head_spec,
                                     pl.BlockSpec((1, tt, LANES), lambda i, j: (i, j, 0))],
        out_shape=[tile(BF16), tile(BF16), tile(F32), tile(F32), tile(BF16),
                   jax.ShapeDtypeStruct((b, nh, t // tt, rows, tt), BF16),
                   jax.ShapeDtypeStruct((b, t, nh), F32), jax.ShapeDtypeStruct((b, t, nh), F32),
                   jax.ShapeDtypeStruct((b, t, LANES), BF16)],
        scratch_shapes=[pltpu.VMEM((1, LANES), F32)],
        compiler_params=_params(2),
        name="fox_in",
    )(x, mod, gpre, w_main, w_vt, w_f3, b_f3, _tri(tt))


def _cumsum_kernel(x_ref, tri_ref, o_ref, car):
    @pl.when(pl.program_id(1) == 0)
    def _():
        car[...] = jnp.zeros_like(car)

    c = _tri_cumsum(tri_ref[...], x_ref[0], car[...])
    o_ref[0] = c
    tc = c.shape[0]
    car[...] = c[tc - 1:tc, :]


def _cumsum_time(x, *, tc):
    b, t, nh = x.shape
    return pl.pallas_call(
        _cumsum_kernel,
        grid=(b, t // tc),
        in_specs=[pl.BlockSpec((1, tc, nh), lambda i, j: (i, j, 0)), _const_spec((tc, tc))],
        out_specs=pl.BlockSpec((1, tc, nh), lambda i, j: (i, j, 0)),
        out_shape=jax.ShapeDtypeStruct((b, t, nh), F32),
        scratch_shapes=[pltpu.VMEM((1, nh), F32)],
        compiler_params=_params(2),
        name="cumsum_time",
    )(x, _tri(tc))


def _split_heads(q):
    qf = q.astype(F32)
    lane = lax.broadcasted_iota(jnp.int32, qf.shape, 1)
    return (jnp.where(lane < HEAD_DIM, qf, 0.0).astype(BF16),
            jnp.where(lane >= HEAD_DIM, qf, 0.0).astype(BF16))


def _attn_kernel(q_ref, k_ref, a_ref, vt_ref, o_ref, z0, z1, zm0, zm1, p0, p1, al0, al1, m_s, acc_s,
                 *, tq, nh):
    hp = pl.program_id(1)
    qi = pl.program_id(2)
    z_s, zmax_s, p_s, alpha_s = (z0, z1), (zm0, zm1), (p0, p1), (al0, al1)

    @pl.when((pl.program_id(0) == 0) & (hp == 0) & (qi == 0))
    def _():
        for p_ref, al_ref in zip(p_s, alpha_s):
            p_ref[...] = jnp.zeros_like(p_ref)
            al_ref[...] = jnp.ones_like(al_ref)

    q = q_ref[0].astype(F32)
    lane = lax.broadcasted_iota(jnp.int32, (tq, LANES), 1)
    qa = []
    for hh in range(2):
        head = 2 * hp + hh
        own = (lane >= hh * HEAD_DIM) & (lane < (hh + 1) * HEAD_DIM)
        pick = (lane == head) | (lane == head + nh) | (lane == head + 2 * nh)
        qa.append(jnp.concatenate(
            [jnp.where(own, q, 0.0).astype(BF16), jnp.where(pick, 1.0, 0.0).astype(BF16)], axis=1))
    m_s[...] = jnp.full_like(m_s, NEG_BIG)
    acc_s[...] = jnp.zeros_like(acc_s)

    def qk(blk, slot, masked):
        st = pl.multiple_of(blk * tq, tq)
        keys = jnp.concatenate([k_ref[0, pl.ds(st, tq), :], a_ref[0, pl.ds(st, tq), :]], axis=1)
        for hh in range(2):
            z = lax.dot_general(keys, qa[hh], _NT, preferred_element_type=F32)
            if masked:
                kpos = lax.broadcasted_iota(jnp.int32, z.shape, 0)
                qpos = lax.broadcasted_iota(jnp.int32, z.shape, 1)
                z = jnp.where(kpos <= qpos, z, NEG_BIG)
            z_s[slot][hh] = z
            zmax_s[slot][hh] = jnp.max(z, axis=0, keepdims=True)

    def softmax(slot):
        for hh in range(2):
            m_prev = m_s[hh]
            m_next = jnp.maximum(m_prev, zmax_s[slot][hh])
            p_s[slot][hh] = jnp.exp2(z_s[slot][hh] - m_next).astype(BF16)
            alpha_s[slot][hh] = jnp.exp2(m_prev - m_next)
            m_s[hh] = m_next

    tv = vt_ref.shape[4]
    nsub = tq // tv

    def pv(blk, slot, valid):
        for hh in range(2):
            r = None
            for u in range(nsub):
                part = jnp.dot(vt_ref[0, hh, blk * nsub + u], p_s[slot][hh, u * tv:(u + 1) * tv, :],
                               preferred_element_type=F32)
                r = part if r is None else r + part
            if valid is not None:
                r = jnp.where(valid, r, 0.0)
            acc_s[hh] = acc_s[hh] * alpha_s[slot][hh] + r

    def iteration(i, slot):
        pv(jnp.minimum(qi - i + 2, qi), slot, i >= 2)
        softmax(slot)
        qk(qi - i - 1, 1 - slot, False)

    def finish(last):
        pv(jnp.minimum(2, qi), last, qi >= 2)
        softmax(last)
        pv(jnp.minimum(1, qi), 1 - last, qi >= 1)
        pv(0, last, None)
        outs = []
        for hh in range(2):
            acc = acc_s[hh]
            outs.append(acc[:HEAD_DIM] / acc[HEAD_DIM:HEAD_DIM + 1])
        o_ref[0] = jnp.concatenate(outs, axis=0).T.astype(o_ref.dtype)

    qk(qi, 0, True)

    def pair(k, carry):
        iteration(2 * k, 0)
        iteration(2 * k + 1, 1)
        return carry

    lax.fori_loop(0, qi // 2, pair, 0)
    odd = lax.rem(qi, 2) == 1

    @pl.when(odd)
    def _():
        iteration(qi - 1, 0)
        finish(1)

    @pl.when(jnp.logical_not(odd))
    def _():
        finish(0)


def _attention(qb, kb, aug, vt, nh, *, tq):
    b, t, w = qb.shape
    hp = w // LANES
    nq = t // tq
    _, _, nv, rows, tv = vt.shape
    kern = functools.partial(_attn_kernel, tq=tq, nh=nh)
    return pl.pallas_call(
        kern,
        grid=(b, hp, nq),
        in_specs=[
            pl.BlockSpec((1, tq, LANES), lambda i, h, j: (i, j, h)),
            pl.BlockSpec((1, t, LANES), lambda i, h, j: (i, 0, h)),
            pl.BlockSpec((1, t, LANES), lambda i, h, j: (i, 0, 0)),
            pl.BlockSpec((1, 2, nv, rows, tv), lambda i, h, j: (i, h, 0, 0, 0)),
        ],
        out_specs=pl.BlockSpec((1, tq, LANES), lambda i, h, j: (i, j, h)),
        out_shape=jax.ShapeDtypeStruct((b, t, w), BF16),
        scratch_shapes=(
            [pltpu.VMEM((2, tq, tq), F32)] * 2
            + [pltpu.VMEM((2, 1, tq), F32)] * 2
            + [pltpu.VMEM((2, tq, tq), BF16)] * 2
            + [pltpu.VMEM((2, 1, tq), F32)] * 2
            + [pltpu.VMEM((2, 1, tq), F32),
               pltpu.VMEM((2, rows, tq), F32)]),
        compiler_params=_params(3),
        name="fox_attention",
    )(qb, kb, aug, vt)


def _attn_cached_kernel(q_ref, kn_ref, vn_ref, pk_ref, pv_ref, cp_ref, cn_ref, o_ref):
    qh = _split_heads(q_ref[0])
    pkb = pk_ref[0]
    pvb = pv_ref[0]
    kn = kn_ref[0]
    vn = vn_ref[0]
    t = kn.shape[0]
    p_len = pkb.shape[0]
    rowi = lax.broadcasted_iota(jnp.int32, (t, t), 0)
    coli = lax.broadcasted_iota(jnp.int32, (t, t), 1)
    outs = []
    for hh in range(2):
        cp = cp_ref[0, 0, hh:hh + 1, :]
        tot = cp[:, p_len - 1:p_len]
        z1 = lax.dot_general(qh[hh], pkb, _NT, preferred_element_type=F32) - (cp - tot) * LOG2E
        z2 = (lax.dot_general(qh[hh], kn, _NT, preferred_element_type=F32)
              - cn_ref[0, 0, hh:hh + 1, :] * LOG2E)
        z2 = jnp.where(coli <= rowi, z2, NEG_BIG)
        m = jnp.maximum(jnp.max(z1, axis=1), jnp.max(z2, axis=1))[:, None]
        p1 = jnp.exp2(z1 - m)
        p2 = jnp.exp2(z2 - m)
        l = (jnp.sum(p1, axis=1) + jnp.sum(p2, axis=1))[:, None]
        o = (jnp.dot(p1.astype(BF16), pvb, preferred_element_type=F32)
             + jnp.dot(p2.astype(BF16), vn, preferred_element_type=F32))
        outs.append(o / l)
    lane = lax.broadcasted_iota(jnp.int32, (t, LANES), 1)
    o_ref[0] = jnp.where(lane < HEAD_DIM, outs[0], outs[1]).astype(o_ref.dtype)


def _attention_cached(qb, kb, vb, past_k, past_v, cum_past, cum_new):
    b, t, w = qb.shape
    p_len = past_k.shape[1]
    hp = w // LANES
    new_spec = pl.BlockSpec((1, t, LANES), lambda i, h: (i, 0, h))
    past_spec = pl.BlockSpec((1, p_len, LANES), lambda i, h: (i, 0, h))
    return pl.pallas_call(
        _attn_cached_kernel,
        grid=(b, hp),
        in_specs=[new_spec, new_spec, new_spec, past_spec, past_spec,
                  pl.BlockSpec((1, 1, 2, p_len), lambda i, h: (i, h, 0, 0)),
                  pl.BlockSpec((1, 1, 2, t), lambda i, h: (i, h, 0, 0))],
        out_specs=new_spec,
        out_shape=jax.ShapeDtypeStruct((b, t, w), BF16),
        compiler_params=_params(2),
        name="fox_attention_cached",
    )(qb, kb, vb, past_k, past_v,
      cum_past.reshape(b, hp, 2, p_len), cum_new.reshape(b, hp, 2, t))


def _fox_out_kernel(o_ref, sg_ref, x_ref, mod_ref, gpost_ref, w_ref, y_ref):
    yv = (o_ref[0].astype(F32) * sg_ref[0].astype(F32)).astype(BF16)
    y = jnp.dot(yv, w_ref[...], preferred_element_type=F32)
    y_ref[0] = _gated_residual(x_ref[0], y, mod_ref[0], gpost_ref[...])


def _fox_out(o, g, x, mod, mod_off, gpost, w_out, *, tt):
    b, t, d = x.shape
    w = o.shape[-1]
    return pl.pallas_call(
        _fox_out_kernel,
        grid=(b, t // tt),
        in_specs=[
            pl.BlockSpec((1, tt, w), lambda i, j: (i, j, 0)),
            pl.BlockSpec((1, tt, w), lambda i, j: (i, j, 0)),
            pl.BlockSpec((1, tt, d), lambda i, j: (i, j, 0)),
            pl.BlockSpec((1, 3, d), lambda i, j: (i + mod_off, 0, 0)),
            _const_spec((1, d)),
            _const_spec((w, d)),
        ],
        out_specs=pl.BlockSpec((1, tt, d), lambda i, j: (i, j, 0)),
        out_shape=jax.ShapeDtypeStruct((b, t, d), F32),
        compiler_params=_params(2),
        name="fox_out",
    )(o, g, x, mod, gpost, w_out)


def _tile(t, pref):
    return pref if t % pref == 0 else t


def kernel(x_prompt, x_sample, c_prompt, c_sample, state_lru_h, state_lru_conv, cache_fox_k, cache_fox_v, cache_fox_logf, norm_pre, norm_post, ada_w, ada_b, lru_w_in, lru_conv_w, lru_conv_b, lru_w_a, lru_b_a, lru_w_x, lru_b_x, lru_lambda, lru_w_out, fox_w_in, fox_b_f, fox_w_out):
    b, t, d = x_prompt.shape
    db, dt, _ = x_sample.shape
    depth = ada_w.shape[0]
    r = lru_lambda.shape[-1]
    nh = fox_b_f.shape[-1]
    w = fox_w_out.shape[1]
    p_len = cache_fox_k.shape[2]
    scale = float(HEAD_DIM) ** -0.5

    c_all = jnp.concatenate([c_prompt, c_sample], axis=0)
    mod = _adaln(c_all, ada_w, ada_b).reshape(depth, b + db, 3, d)

    xp, xs = x_prompt, x_sample

    j = 0
    w_ax = jnp.concatenate([lru_w_a[j], lru_w_x[j]], axis=-1).astype(BF16)
    lru_args = (norm_pre[0:1], norm_post[0:1], lru_w_in[j].astype(BF16), lru_conv_w[j],
                lru_conv_b[j:j + 1], w_ax, lru_b_a[j:j + 1], lru_b_x[j:j + 1],
                lru_lambda[j:j + 1], lru_w_out[j].astype(BF16))
    xp, conv_p, h_p = _lru_layer(
        xp, mod[0], 0, *lru_args,
        jnp.zeros((b, CONV_TAPS - 1, r), F32), jnp.zeros((b, 1, r), F32), tt=_tile(t, 512))
    xs, conv_s, h_s = _lru_layer(
        xs, mod[0], b, *lru_args,
        state_lru_conv[j], state_lru_h[j].reshape(db, 1, r), tt=_tile(dt, 256))

    w_main = fox_w_in[j][:, :4 * w].astype(BF16)
    pad = jnp.zeros((d, LANES - 3 * nh), F32)
    w_f3 = jnp.concatenate([fox_w_in[j][:, 4 * w:]] * 3 + [pad], axis=1).astype(BF16)
    b_f3 = jnp.concatenate([fox_b_f[j]] * 3 + [jnp.zeros((LANES - 3 * nh,), F32)]).reshape(1, LANES)
    w_out = fox_w_out[j].astype(BF16)
    qscale = scale * LOG2E

    w_vt = fox_w_in[j][:, 2 * w:3 * w].T.astype(BF16)
    qb, kb, k_p, v_p, sg_p, vt_p, lf_p, _, aug_p = _fox_in(
        xp, mod[1], 0, norm_pre[1:2], w_main, w_vt, w_f3, b_f3, nh, tt=_tile(t, 256), qscale=qscale)
    o_p = _attention(qb, kb, aug_p, vt_p, nh, tq=_tile(t, 512))
    yp = _fox_out(o_p, sg_p, xp, mod[1], 0, norm_post[1:2], w_out, tt=_tile(t, 512))

    qs, ksb, k_s, v_s, sg_s, _, lf_s, cum_s, _ = _fox_in(
        xs, mod[1], b, norm_pre[1:2], w_main, w_vt, w_f3, b_f3, nh, tt=_tile(dt, 256), qscale=qscale)
    lf_past = cache_fox_logf[j].transpose(1, 0, 2).reshape(1, p_len, db * nh)
    cum_past = _cumsum_time(lf_past, tc=_tile(p_len, 512)).reshape(p_len, db, nh).transpose(1, 2, 0)
    o_s = _attention_cached(qs, ksb, v_s.astype(BF16),
                            cache_fox_k[j].reshape(db, p_len, w).astype(BF16),
                            cache_fox_v[j].reshape(db, p_len, w).astype(BF16),
                            cum_past, jnp.swapaxes(cum_s, 1, 2))
    ys = _fox_out(o_s, sg_s, xs, mod[1], b, norm_post[1:2], w_out, tt=_tile(dt, 512))

    hd = (nh, HEAD_DIM)
    return (yp, ys,
            h_p.reshape(1, b, r), conv_p[None],
            k_p.reshape(1, b, t, *hd), v_p.reshape(1, b, t, *hd), lf_p[None],
            h_s.reshape(1, db, r), conv_s[None],
            k_s.reshape(1, db, dt, *hd), v_s.reshape(1, db, dt, *hd), lf_s[None])
```

```python
import functools

import jax
import jax.numpy as jnp
from jax import lax
from jax.experimental import pallas as pl
from jax.experimental.pallas import tpu as pltpu

F32 = jnp.float32
BF16 = jnp.bfloat16

EPS = 1e-6
LRU_C = 8.0
CONV_TAPS = 4
HEAD_DIM = 64
LANES = 128
SUBLANES = 8
NEG_BIG = -1e30
LOG2E = 1.4426950408889634
ONES_ROWS = 16
VMEM_LIMIT_BYTES = 56 * 1024 * 1024

_NT = (((1,), (1,)), ((), ()))


def _params(n_grid):
    return pltpu.CompilerParams(
        dimension_semantics=("arbitrary",) * n_grid,
        vmem_limit_bytes=VMEM_LIMIT_BYTES)


def _const_spec(shape):
    nd = len(shape)
    return pl.BlockSpec(shape, lambda *_: (0,) * nd, pipeline_mode=pl.Buffered(1))


def _sigmoid(x):
    return 0.5 * jnp.tanh(0.5 * x) + 0.5


def _silu(x):
    return x * _sigmoid(x)


def _softplus(x):
    return jnp.maximum(x, 0.0) + jnp.log1p(jnp.exp(-jnp.abs(x)))


def _one_minus_exp_neg(t, exp_neg_t):
    p = jnp.full_like(t, -1.0 / 24.0)
    for c in (1.0 / 6.0, -0.5, 1.0):
        p = p * t + c
    return jnp.where(t < 1.0 / 64.0, t * p, 1.0 - exp_neg_t)


def _sqrt_pos(x):
    x = jnp.maximum(x, 1e-30)
    return x * lax.rsqrt(x)


def _modulated_norm(x, mod, g):
    ms = jnp.mean(x * x, axis=-1, keepdims=True)
    h = x * lax.rsqrt(ms + EPS) * g
    return h * (1.0 + mod[1:2]) + mod[0:1]


def _gated_residual(x, y, mod, g):
    ms = jnp.mean(y * y, axis=-1, keepdims=True)
    return x + mod[2:3] * (y * lax.rsqrt(ms + EPS) * g)


def _adaln_kernel(c_ref, w_ref, b_ref, o_ref):
    sc = _silu(c_ref[...]).astype(BF16)
    o_ref[0] = jnp.dot(sc, w_ref[0].astype(BF16), preferred_element_type=F32) + b_ref[0]


def _adaln(c_all, ada_w, ada_b):
    depth, d, d3 = ada_w.shape
    n = c_all.shape[0]
    nj = d3 // d
    return pl.pallas_call(
        _adaln_kernel,
        grid=(depth, nj),
        in_specs=[
            pl.BlockSpec((n, d), lambda i, j: (0, 0)),
            pl.BlockSpec((1, d, d), lambda i, j: (i, 0, j)),
            pl.BlockSpec((1, 1, d), lambda i, j: (i, 0, j)),
        ],
        out_specs=pl.BlockSpec((1, n, d), lambda i, j: (i, 0, j)),
        out_shape=jax.ShapeDtypeStruct((depth, n, d3), F32),
        compiler_params=_params(2),
        name="adaln",
    )(c_all, ada_w, ada_b.reshape(depth, 1, d3))


def _lru_kernel(x_ref, mod_ref, gpre_ref, gpost_ref, win_ref, cw_ref, cb_ref, wax_ref,
                ba_ref, bx_ref, lam_ref, wout_ref, conv0_ref, h0_ref,
                y_ref, convo_ref, hlast_ref,
                xbuf, xc_s, pre_a, pre_x, gate_s, yv_s, hcar, *, tt, r, nblk):
    t = pl.program_id(1)
    nt = pl.num_programs(1)
    tail0 = SUBLANES - (CONV_TAPS - 1)

    @pl.when(t == 0)
    def _():
        xbuf[tail0:SUBLANES, :] = conv0_ref[0]
        hcar[...] = h0_ref[0]

    x = x_ref[0]
    mod = mod_ref[0]
    h = _modulated_norm(x, mod, gpre_ref[...])
    z = jnp.dot(h.astype(BF16), win_ref[...], preferred_element_type=F32)
    xbuf[SUBLANES:SUBLANES + tt, :] = z[:, :r]
    gate_s[...] = z[:, r:]

    cw = cw_ref[...]
    xc = cb_ref[...] + xbuf[tail0:tail0 + tt, :] * cw[0:1]
    for k in range(1, CONV_TAPS):
        xc = xc + xbuf[tail0 + k:tail0 + k + tt, :] * cw[k:k + 1]
    tail = xbuf[tt + tail0:tt + SUBLANES, :]
    xbuf[tail0:SUBLANES, :] = tail
    xc_s[...] = xc

    @pl.when(t == nt - 1)
    def _():
        convo_ref[0] = tail

    xcb = xc.astype(BF16)
    for n in range(nblk):
        lo, hi = n * LANES, (n + 1) * LANES
        res = jnp.dot(xcb[:, lo:hi], wax_ref[n], preferred_element_type=F32)
        pre_a[:, lo:hi] = res[:, :LANES]
        pre_x[:, lo:hi] = res[:, LANES:]

    coef = -LRU_C * _softplus(-lam_ref[...])
    coef2n = -2.0 * coef
    ba = ba_ref[...]
    bx = bx_ref[...]
    row = lax.broadcasted_iota(jnp.int32, (SUBLANES, r), 0)

    def group(g, hprev):
        st = pl.multiple_of(g * SUBLANES, SUBLANES)
        sl = pl.ds(st, SUBLANES)
        rg = _sigmoid(pre_a[sl, :] + ba)
        ig = _sigmoid(pre_x[sl, :] + bx)
        a = jnp.exp(coef * rg)
        u = _sqrt_pos(_one_minus_exp_neg(coef2n * rg, a * a)) * (ig * xc_s[sl, :])
        for s in (1, 2, 4):
            a_sh = jnp.where(row >= s, pltpu.roll(a, s, axis=0), 1.0)
            u_sh = jnp.where(row >= s, pltpu.roll(u, s, axis=0), 0.0)
            u = a * u_sh + u
            a = a * a_sh
        hs = a * hprev + u
        yv_s[sl, :] = hs * _silu(gate_s[sl, :])
        return hs[SUBLANES - 1:SUBLANES, :]

    hlast = lax.fori_loop(0, tt // SUBLANES, group, hcar[...])
    hcar[...] = hlast

    @pl.when(t == nt - 1)
    def _():
        hlast_ref[0] = hlast

    y = jnp.dot(yv_s[...].astype(BF16), wout_ref[...], preferred_element_type=F32)
    y_ref[0] = _gated_residual(x, y, mod, gpost_ref[...])


def _lru_layer(x, mod, mod_off, gpre, gpost, w_in, conv_w, conv_b, w_ax, b_a, b_x, lam, w_out,
               conv0, h0, *, tt):
    b, t, d = x.shape
    r = lam.shape[-1]
    nblk = w_ax.shape[0]
    kern = functools.partial(_lru_kernel, tt=tt, r=r, nblk=nblk)
    return pl.pallas_call(
        kern,
        grid=(b, t // tt),
        in_specs=[
            pl.BlockSpec((1, tt, d), lambda i, j: (i, j, 0)),
            pl.BlockSpec((1, 3, d), lambda i, j: (i + mod_off, 0, 0)),
            _const_spec((1, d)), _const_spec((1, d)),
            _const_spec((d, 2 * r)),
            _const_spec((CONV_TAPS, r)), _const_spec((1, r)),
            _const_spec((nblk, LANES, 2 * LANES)),
            _const_spec((1, r)), _const_spec((1, r)), _const_spec((1, r)),
            _const_spec((r, d)),
            pl.BlockSpec((1, CONV_TAPS - 1, r), lambda i, j: (i, 0, 0)),
            pl.BlockSpec((1, 1, r), lambda i, j: (i, 0, 0)),
        ],
        out_specs=[
            pl.BlockSpec((1, tt, d), lambda i, j: (i, j, 0)),
            pl.BlockSpec((1, CONV_TAPS - 1, r), lambda i, j: (i, 0, 0)),
            pl.BlockSpec((1, 1, r), lambda i, j: (i, 0, 0)),
        ],
        out_shape=[
            jax.ShapeDtypeStruct((b, t, d), F32),
            jax.ShapeDtypeStruct((b, CONV_TAPS - 1, r), F32),
            jax.ShapeDtypeStruct((b, 1, r), F32),
        ],
        scratch_shapes=[
            pltpu.VMEM((tt + SUBLANES, r), F32),
            pltpu.VMEM((tt, r), F32), pltpu.VMEM((tt, r), F32), pltpu.VMEM((tt, r), F32),
            pltpu.VMEM((tt, r), F32), pltpu.VMEM((tt, r), F32),
            pltpu.VMEM((1, r), F32),
        ],
        compiler_params=_params(2),
        name="lru_layer",
    )(x, mod, gpre, gpost, w_in, conv_w, conv_b, w_ax, b_a, b_x, lam, w_out, conv0, h0)


def _split3(x):
    x1 = x.astype(BF16)
    r1 = x - x1.astype(F32)
    x2 = r1.astype(BF16)
    x3 = (r1 - x2.astype(F32)).astype(BF16)
    return x1, x2, x3


def _tri_cumsum(tri, x, carry):
    c = carry
    for piece in _split3(x):
        c = c + jnp.dot(tri, piece, preferred_element_type=F32)
    return c


def _fox_in_kernel(x_ref, mod_ref, gpre_ref, w_ref, wvt_ref, wf_ref, bf_ref, tri_ref, hsel_ref,
                   q_ref, kb_ref, k_ref, v_ref, sg_ref, vt_ref, lf_ref, cum_ref, aug_ref,
                   qn_ref, kn_ref, car, *, w, nh, qscale):
    @pl.when(pl.program_id(1) == 0)
    def _():
        car[...] = jnp.zeros_like(car)

    hb = _modulated_norm(x_ref[0], mod_ref[0], gpre_ref[...]).astype(BF16)
    z = jnp.dot(hb, w_ref[...], preferred_element_type=F32)
    q = z[:, :w] * qscale
    q_ref[0] = q.astype(BF16)
    k = z[:, w:2 * w]
    k_ref[0] = k
    kb_ref[0] = k.astype(BF16)
    hsel = hsel_ref[...]
    qn_ref[0, 0] = jnp.max(jnp.dot((q * q).astype(BF16), hsel, preferred_element_type=F32),
                           axis=0, keepdims=True)
    kn_ref[0, 0] = jnp.max(jnp.dot((k * k).astype(BF16), hsel, preferred_element_type=F32),
                           axis=0, keepdims=True)
    v_ref[0] = z[:, 2 * w:3 * w]
    sg_ref[0] = _silu(z[:, 3 * w:]).astype(BF16)
    vt = lax.dot_general(wvt_ref[...], hb, _NT, preferred_element_type=F32)
    tt = vt.shape[1]
    vt = vt.reshape(nh, HEAD_DIM, tt)
    vt_ref[0, :, 0] = jnp.concatenate([vt, jnp.ones((nh, ONES_ROWS, tt), F32)], axis=1).astype(BF16)
    fl = jnp.dot(hb, wf_ref[...], preferred_element_type=F32) + bf_ref[...]
    lf = -_softplus(-fl)
    c = _tri_cumsum(tri_ref[...], lf, car[...])
    tt = c.shape[0]
    car[...] = c[tt - 1:tt, :]
    lf_ref[0] = lf[:, :nh]
    cum_ref[0] = c[:, :nh]
    y1, y2, y3 = _split3(c * (-LOG2E))
    lane = lax.broadcasted_iota(jnp.int32, c.shape, 1)
    aug = jnp.where(lane < nh, y1.astype(F32),
                    jnp.where(lane < 2 * nh, y2.astype(F32),
                              jnp.where(lane < 3 * nh, y3.astype(F32), 0.0)))
    aug_ref[0] = aug.astype(BF16)


def _tri(n):
    return jnp.tril(jnp.ones((n, n), F32)).astype(BF16)


def _fox_in(x, mod, mod_off, gpre, w_main, w_vt, w_f3, b_f3, nh, *, tt, qscale):
    b, t, d = x.shape
    w = w_main.shape[1] // 4
    rows = HEAD_DIM + ONES_ROWS
    kern = functools.partial(_fox_in_kernel, w=w, nh=nh, qscale=qscale)
    tile = lambda dt: jax.ShapeDtypeStruct((b, t, w), dt)
    tile_spec = pl.BlockSpec((1, tt, w), lambda i, j: (i, j, 0))
    head_spec = pl.BlockSpec((1, tt, nh), lambda i, j: (i, j, 0))
    norm_spec = pl.BlockSpec((1, 1, 1, LANES), lambda i, j: (i, j, 0, 0))
    hsel = (jnp.arange(w)[:, None] // HEAD_DIM == jnp.arange(LANES)[None, :]).astype(BF16)
    return pl.pallas_call(
        kern,
        grid=(b, t // tt),
        in_specs=[
            pl.BlockSpec((1, tt, d), lambda i, j: (i, j, 0)),
            pl.BlockSpec((1, 3, d), lambda i, j: (i + mod_off, 0, 0)),
            _const_spec((1, d)),
            _const_spec((d, 4 * w)),
            _const_spec((w, d)),
            _const_spec((d, LANES)),
            _const_spec((1, LANES)),
            _const_spec((tt, tt)),
            _const_spec((w, LANES)),
        ],
        out_specs=[tile_spec] * 5 + [pl.BlockSpec((1, nh, 1, rows, tt), lambda i, j: (i, 0, j, 0, 0)),
                                     head_spec, head_spec,
                                     pl.BlockSpec((1, tt, LANES), lambda i, j: (i, j, 0)),
                                     norm_spec, norm_spec],
        out_shape=[tile(BF16), tile(BF16), tile(F32), tile(F32), tile(BF16),
                   jax.ShapeDtypeStruct((b, nh, t // tt, rows, tt), BF16),
                   jax.ShapeDtypeStruct((b, t, nh), F32), jax.ShapeDtypeStruct((b, t, nh), F32),
                   jax.ShapeDtypeStruct((b, t, LANES), BF16),
                   jax.ShapeDtypeStruct((b, t // tt, 1, LANES), F32),
                   jax.ShapeDtypeStruct((b, t // tt, 1, LANES), F32)],
        scratch_shapes=[pltpu.VMEM((1, LANES), F32)],
        compiler_params=_params(2),
        name="fox_in",
    )(x, mod, gpre, w_main, w_vt, w_f3, b_f3, _tri(tt), hsel)


def _cumsum_kernel(x_ref, tri_ref, o_ref, car):
    @pl.when(pl.program_id(1) == 0)
    def _():
        car[...] = jnp.zeros_like(car)

    c = _tri_cumsum(tri_ref[...], x_ref[0], car[...])
    o_ref[0] = c
    tc = c.shape[0]
    car[...] = c[tc - 1:tc, :]


def _cumsum_time(x, *, tc):
    b, t, nh = x.shape
    return pl.pallas_call(
        _cumsum_kernel,
        grid=(b, t // tc),
        in_specs=[pl.BlockSpec((1, tc, nh), lambda i, j: (i, j, 0)), _const_spec((tc, tc))],
        out_specs=pl.BlockSpec((1, tc, nh), lambda i, j: (i, j, 0)),
        out_shape=jax.ShapeDtypeStruct((b, t, nh), F32),
        scratch_shapes=[pltpu.VMEM((1, nh), F32)],
        compiler_params=_params(2),
        name="cumsum_time",
    )(x, _tri(tc))


def _split_heads(q):
    qf = q.astype(F32)
    lane = lax.broadcasted_iota(jnp.int32, qf.shape, 1)
    return (jnp.where(lane < HEAD_DIM, qf, 0.0).astype(BF16),
            jnp.where(lane >= HEAD_DIM, qf, 0.0).astype(BF16))


def _attn_kernel(cnt_ref, q_ref, k_ref, a_ref, vt_ref, o_ref, z0, z1, zm0, zm1, p0, p1, al0, al1,
                 m_s, acc_s, *, tq, nh):
    bi = pl.program_id(0)
    hp = pl.program_id(1)
    qi = pl.program_id(2)
    nq = pl.num_programs(2)
    z_s, zmax_s, p_s, alpha_s = (z0, z1), (zm0, zm1), (p0, p1), (al0, al1)

    @pl.when((bi == 0) & (hp == 0) & (qi == 0))
    def _():
        for p_ref, al_ref in zip(p_s, alpha_s):
            p_ref[...] = jnp.zeros_like(p_ref)
            al_ref[...] = jnp.ones_like(al_ref)

    q = q_ref[0].astype(F32)
    lane = lax.broadcasted_iota(jnp.int32, (tq, LANES), 1)
    tv = vt_ref.shape[4]
    nsub = tq // tv

    def head_pass(hh):
        head = 2 * hp + hh
        n = cnt_ref[(bi * nh + head) * nq + qi]
        own = (lane >= hh * HEAD_DIM) & (lane < (hh + 1) * HEAD_DIM)
        pick = (lane == head) | (lane == head + nh) | (lane == head + 2 * nh)
        qa = jnp.concatenate(
            [jnp.where(own, q, 0.0).astype(BF16), jnp.where(pick, 1.0, 0.0).astype(BF16)], axis=1)
        m_s[...] = jnp.full_like(m_s, NEG_BIG)
        acc_s[hh] = jnp.zeros(acc_s.shape[1:], F32)

        def qk(blk, slot, masked):
            st = pl.multiple_of(blk * tq, tq)
            keys = jnp.concatenate([k_ref[0, pl.ds(st, tq), :], a_ref[0, pl.ds(st, tq), :]], axis=1)
            z = lax.dot_general(keys, qa, _NT, preferred_element_type=F32)
            if masked:
                kpos = lax.broadcasted_iota(jnp.int32, z.shape, 0)
                qpos = lax.broadcasted_iota(jnp.int32, z.shape, 1)
                z = jnp.where(kpos <= qpos, z, NEG_BIG)
            z_s[slot][...] = z
            zmax_s[slot][...] = jnp.max(z, axis=0, keepdims=True)

        def softmax(slot):
            m_prev = m_s[...]
            m_next = jnp.maximum(m_prev, zmax_s[slot][...])
            p_s[slot][...] = jnp.exp2(z_s[slot][...] - m_next).astype(BF16)
            alpha_s[slot][...] = jnp.exp2(m_prev - m_next)
            m_s[...] = m_next

        def pv(blk, slot, valid):
            r = None
            for u in range(nsub):
                part = jnp.dot(vt_ref[0, hh, blk * nsub + u], p_s[slot][u * tv:(u + 1) * tv, :],
                               preferred_element_type=F32)
                r = part if r is None else r + part
            if valid is not None:
                r = jnp.where(valid, r, 0.0)
            acc_s[hh] = acc_s[hh] * alpha_s[slot][...] + r

        def iteration(i, slot):
            pv(jnp.minimum(qi - i + 2, qi), slot, i >= 2)
            softmax(slot)
            qk(qi - i - 1, 1 - slot, False)

        def finish(last):
            pv(jnp.minimum(qi - n + 2, qi), last, n >= 2)
            softmax(last)
            pv(jnp.minimum(qi - n + 1, qi), 1 - last, n >= 1)
            pv(qi - n, last, None)

        qk(qi, 0, True)

        def pair(k, carry):
            iteration(2 * k, 0)
            iteration(2 * k + 1, 1)
            return carry

        lax.fori_loop(0, n // 2, pair, 0)
        odd = lax.rem(n, 2) == 1

        @pl.when(odd)
        def _():
            iteration(n - 1, 0)
            finish(1)

        @pl.when(jnp.logical_not(odd))
        def _():
            finish(0)

    outs = []
    for hh in range(2):
        head_pass(hh)
        acc = acc_s[hh]
        outs.append(acc[:HEAD_DIM] / acc[HEAD_DIM:HEAD_DIM + 1])
    o_ref[0] = jnp.concatenate(outs, axis=0).T.astype(o_ref.dtype)


def _attention(cnt, qb, kb, aug, vt, nh, *, tq):
    b, t, w = qb.shape
    hp = w // LANES
    nq = t // tq
    _, _, nv, rows, tv = vt.shape
    kern = functools.partial(_attn_kernel, tq=tq, nh=nh)
    grid_spec = pltpu.PrefetchScalarGridSpec(
        num_scalar_prefetch=1,
        grid=(b, hp, nq),
        in_specs=[
            pl.BlockSpec((1, tq, LANES), lambda i, h, j, c: (i, j, h)),
            pl.BlockSpec((1, t, LANES), lambda i, h, j, c: (i, 0, h)),
            pl.BlockSpec((1, t, LANES), lambda i, h, j, c: (i, 0, 0)),
            pl.BlockSpec((1, 2, nv, rows, tv), lambda i, h, j, c: (i, h, 0, 0, 0)),
        ],
        out_specs=pl.BlockSpec((1, tq, LANES), lambda i, h, j, c: (i, j, h)),
        scratch_shapes=(
            [pltpu.VMEM((tq, tq), F32)] * 2
            + [pltpu.VMEM((1, tq), F32)] * 2
            + [pltpu.VMEM((tq, tq), BF16)] * 2
            + [pltpu.VMEM((1, tq), F32)] * 2
            + [pltpu.VMEM((1, tq), F32),
               pltpu.VMEM((2, rows, tq), F32)]))
    return pl.pallas_call(
        kern,
        grid_spec=grid_spec,
        out_shape=jax.ShapeDtypeStruct((b, t, w), BF16),
        compiler_params=_params(3),
        name="fox_attention",
    )(cnt, qb, kb, aug, vt)


EXP2_ZERO_BELOW = 150.0


def _plan_kernel(qn_ref, kn_ref, cend_ref, cnt_ref, *, nh):
    qn2 = jnp.max(qn_ref[0], axis=1)[:, :nh]
    kn2 = jnp.max(kn_ref[0], axis=0, keepdims=True)[:, :nh]
    bend = cend_ref[0] * (-LOG2E)
    nq = bend.shape[0]
    bprev = jnp.concatenate([jnp.zeros((1, nh), F32), bend[:nq - 1]], axis=0)
    slack = 2.0 + 1e-5 * jnp.abs(bprev) + 1e-5 * jnp.abs(bend)
    thr = EXP2_ZERO_BELOW + 2.04 * jnp.sqrt(qn2 * kn2) + slack
    blk = lax.broadcasted_iota(jnp.int32, (nq, nh), 0)
    rows = []
    for qi in range(nq):
        live = (blk < qi) & (bprev[qi:qi + 1] - bend <= thr[qi:qi + 1])
        rows.append(jnp.max(jnp.where(live, qi - blk, 0), axis=0, keepdims=True))
    cnt_ref[0] = jnp.concatenate(rows, axis=0)


def _plan(qn2, kn2, cend, nh):
    b, nq, sub, _ = qn2.shape
    nt = kn2.shape[1]
    return pl.pallas_call(
        functools.partial(_plan_kernel, nh=nh),
        grid=(b,),
        in_specs=[
            pl.BlockSpec((1, nq, sub, LANES), lambda i: (i, 0, 0, 0)),
            pl.BlockSpec((1, nt, LANES), lambda i: (i, 0, 0)),
            pl.BlockSpec((1, nq, nh), lambda i: (i, 0, 0)),
        ],
        out_specs=pl.BlockSpec((1, nq, nh), lambda i: (i, 0, 0)),
        out_shape=jax.ShapeDtypeStruct((b, nq, nh), jnp.int32),
        compiler_params=_params(1),
        name="fox_plan",
    )(qn2, kn2, cend)


def _attn_cached_kernel(q_ref, kn_ref, vn_ref, pk_ref, pv_ref, cp_ref, cn_ref, o_ref):
    qh = _split_heads(q_ref[0])
    pkb = pk_ref[0]
    pvb = pv_ref[0]
    kn = kn_ref[0]
    vn = vn_ref[0]
    t = kn.shape[0]
    p_len = pkb.shape[0]
    rowi = lax.broadcasted_iota(jnp.int32, (t, t), 0)
    coli = lax.broadcasted_iota(jnp.int32, (t, t), 1)
    outs = []
    for hh in range(2):
        cp = cp_ref[0, 0, hh:hh + 1, :]
        tot = cp[:, p_len - 1:p_len]
        z1 = lax.dot_general(qh[hh], pkb, _NT, preferred_element_type=F32) - (cp - tot) * LOG2E
        z2 = (lax.dot_general(qh[hh], kn, _NT, preferred_element_type=F32)
              - cn_ref[0, 0, hh:hh + 1, :] * LOG2E)
        z2 = jnp.where(coli <= rowi, z2, NEG_BIG)
        m = jnp.maximum(jnp.max(z1, axis=1), jnp.max(z2, axis=1))[:, None]
        p1 = jnp.exp2(z1 - m)
        p2 = jnp.exp2(z2 - m)
        l = (jnp.sum(p1, axis=1) + jnp.sum(p2, axis=1))[:, None]
        o = (jnp.dot(p1.astype(BF16), pvb, preferred_element_type=F32)
             + jnp.dot(p2.astype(BF16), vn, preferred_element_type=F32))
        outs.append(o / l)
    lane = lax.broadcasted_iota(jnp.int32, (t, LANES), 1)
    o_ref[0] = jnp.where(lane < HEAD_DIM, outs[0], outs[1]).astype(o_ref.dtype)


def _attention_cached(qb, kb, vb, past_k, past_v, cum_past, cum_new):
    b, t, w = qb.shape
    p_len = past_k.shape[1]
    hp = w // LANES
    new_spec = pl.BlockSpec((1, t, LANES), lambda i, h: (i, 0, h))
    past_spec = pl.BlockSpec((1, p_len, LANES), lambda i, h: (i, 0, h))
    return pl.pallas_call(
        _attn_cached_kernel,
        grid=(b, hp),
        in_specs=[new_spec, new_spec, new_spec, past_spec, past_spec,
                  pl.BlockSpec((1, 1, 2, p_len), lambda i, h: (i, h, 0, 0)),
                  pl.BlockSpec((1, 1, 2, t), lambda i, h: (i, h, 0, 0))],
        out_specs=new_spec,
        out_shape=jax.ShapeDtypeStruct((b, t, w), BF16),
        compiler_params=_params(2),
        name="fox_attention_cached",
    )(qb, kb, vb, past_k, past_v,
      cum_past.reshape(b, hp, 2, p_len), cum_new.reshape(b, hp, 2, t))


def _fox_out_kernel(o_ref, sg_ref, x_ref, mod_ref, gpost_ref, w_ref, y_ref):
    yv = (o_ref[0].astype(F32) * sg_ref[0].astype(F32)).astype(BF16)
    y = jnp.dot(yv, w_ref[...], preferred_element_type=F32)
    y_ref[0] = _gated_residual(x_ref[0], y, mod_ref[0], gpost_ref[...])


def _fox_out(o, g, x, mod, mod_off, gpost, w_out, *, tt):
    b, t, d = x.shape
    w = o.shape[-1]
    return pl.pallas_call(
        _fox_out_kernel,
        grid=(b, t // tt),
        in_specs=[
            pl.BlockSpec((1, tt, w), lambda i, j: (i, j, 0)),
            pl.BlockSpec((1, tt, w), lambda i, j: (i, j, 0)),
            pl.BlockSpec((1, tt, d), lambda i, j: (i, j, 0)),
            pl.BlockSpec((1, 3, d), lambda i, j: (i + mod_off, 0, 0)),
            _const_spec((1, d)),
            _const_spec((w, d)),
        ],
        out_specs=pl.BlockSpec((1, tt, d), lambda i, j: (i, j, 0)),
        out_shape=jax.ShapeDtypeStruct((b, t, d), F32),
        compiler_params=_params(2),
        name="fox_out",
    )(o, g, x, mod, gpost, w_out)


def _tile(t, pref):
    return pref if t % pref == 0 else t


def kernel(x_prompt, x_sample, c_prompt, c_sample, state_lru_h, state_lru_conv, cache_fox_k, cache_fox_v, cache_fox_logf, norm_pre, norm_post, ada_w, ada_b, lru_w_in, lru_conv_w, lru_conv_b, lru_w_a, lru_b_a, lru_w_x, lru_b_x, lru_lambda, lru_w_out, fox_w_in, fox_b_f, fox_w_out):
    b, t, d = x_prompt.shape
    db, dt, _ = x_sample.shape
    depth = ada_w.shape[0]
    r = lru_lambda.shape[-1]
    nh = fox_b_f.shape[-1]
    w = fox_w_out.shape[1]
    p_len = cache_fox_k.shape[2]
    scale = float(HEAD_DIM) ** -0.5

    c_all = jnp.concatenate([c_prompt, c_sample], axis=0)
    mod = _adaln(c_all, ada_w, ada_b).reshape(depth, b + db, 3, d)

    xp, xs = x_prompt, x_sample

    j = 0
    w_ax = jnp.concatenate([lru_w_a[j], lru_w_x[j]], axis=-1).astype(BF16)
    lru_args = (norm_pre[0:1], norm_post[0:1], lru_w_in[j].astype(BF16), lru_conv_w[j],
                lru_conv_b[j:j + 1], w_ax, lru_b_a[j:j + 1], lru_b_x[j:j + 1],
                lru_lambda[j:j + 1], lru_w_out[j].astype(BF16))
    xp, conv_p, h_p = _lru_layer(
        xp, mod[0], 0, *lru_args,
        jnp.zeros((b, CONV_TAPS - 1, r), F32), jnp.zeros((b, 1, r), F32), tt=_tile(t, 512))
    xs, conv_s, h_s = _lru_layer(
        xs, mod[0], b, *lru_args,
        state_lru_conv[j], state_lru_h[j].reshape(db, 1, r), tt=_tile(dt, 256))

    w_main = fox_w_in[j][:, :4 * w].astype(BF16)
    pad = jnp.zeros((d, LANES - 3 * nh), F32)
    w_f3 = jnp.concatenate([fox_w_in[j][:, 4 * w:]] * 3 + [pad], axis=1).astype(BF16)
    b_f3 = jnp.concatenate([fox_b_f[j]] * 3 + [jnp.zeros((LANES - 3 * nh,), F32)]).reshape(1, LANES)
    w_out = fox_w_out[j].astype(BF16)
    qscale = scale * LOG2E

    w_vt = fox_w_in[j][:, 2 * w:3 * w].T.astype(BF16)
    tt = _tile(t, 256)
    tq = _tile(t, 512)
    qb, kb, k_p, v_p, sg_p, vt_p, lf_p, cum_p, aug_p, qn_p, kn_p = _fox_in(
        xp, mod[1], 0, norm_pre[1:2], w_main, w_vt, w_f3, b_f3, nh, tt=tt, qscale=qscale)
    cnt = _plan(qn_p.reshape(b, t // tq, tq // tt, LANES), kn_p.reshape(b, t // tt, LANES),
                cum_p[:, tq - 1::tq, :], nh)
    o_p = _attention(cnt.transpose(0, 2, 1).reshape(-1), qb, kb, aug_p, vt_p, nh, tq=tq)
    yp = _fox_out(o_p, sg_p, xp, mod[1], 0, norm_post[1:2], w_out, tt=_tile(t, 512))

    qs, ksb, k_s, v_s, sg_s, _, lf_s, cum_s, _, _, _ = _fox_in(
        xs, mod[1], b, norm_pre[1:2], w_main, w_vt, w_f3, b_f3, nh, tt=_tile(dt, 256), qscale=qscale)
    lf_past = cache_fox_logf[j].transpose(1, 0, 2).reshape(1, p_len, db * nh)
    cum_past = _cumsum_time(lf_past, tc=_tile(p_len, 512)).reshape(p_len, db, nh).transpose(1, 2, 0)
    o_s = _attention_cached(qs, ksb, v_s.astype(BF16),
                            cache_fox_k[j].astype(BF16).reshape(db, p_len, w),
                            cache_fox_v[j].astype(BF16).reshape(db, p_len, w),
                            cum_past, jnp.swapaxes(cum_s, 1, 2))
    ys = _fox_out(o_s, sg_s, xs, mod[1], b, norm_post[1:2], w_out, tt=_tile(dt, 512))

    hd = (nh, HEAD_DIM)
    return (yp, ys,
            h_p.reshape(1, b, r), conv_p[None],
            k_p.reshape(1, b, t, *hd), v_p.reshape(1, b, t, *hd), lf_p[None],
            h_s.reshape(1, db, r), conv_s[None],
            k_s.reshape(1, db, dt, *hd), v_s.reshape(1, db, dt, *hd), lf_s[None])
```

```python
import functools

import jax
import jax.numpy as jnp
from jax import lax
from jax.experimental import pallas as pl
from jax.experimental.pallas import tpu as pltpu

F32 = jnp.float32
BF16 = jnp.bfloat16

EPS = 1e-6
LRU_C = 8.0
CONV_TAPS = 4
HEAD_DIM = 64
LANES = 128
SUBLANES = 8
NEG_BIG = -1e30
LOG2E = 1.4426950408889634
ONES_ROWS = 16
VMEM_LIMIT_BYTES = 56 * 1024 * 1024

_NT = (((1,), (1,)), ((), ()))


def _params(n_grid):
    return pltpu.CompilerParams(
        dimension_semantics=("arbitrary",) * n_grid,
        vmem_limit_bytes=VMEM_LIMIT_BYTES)


def _const_spec(shape):
    nd = len(shape)
    return pl.BlockSpec(shape, lambda *_: (0,) * nd, pipeline_mode=pl.Buffered(1))


def _sigmoid(x):
    return 0.5 * jnp.tanh(0.5 * x) + 0.5


def _silu(x):
    return x * _sigmoid(x)


def _softplus(x):
    return jnp.maximum(x, 0.0) + jnp.log1p(jnp.exp(-jnp.abs(x)))


def _one_minus_exp_neg(t, exp_neg_t):
    p = jnp.full_like(t, -1.0 / 24.0)
    for c in (1.0 / 6.0, -0.5, 1.0):
        p = p * t + c
    return jnp.where(t < 1.0 / 64.0, t * p, 1.0 - exp_neg_t)


def _sqrt_pos(x):
    x = jnp.maximum(x, 1e-30)
    return x * lax.rsqrt(x)


def _modulated_norm(x, mod, g):
    ms = jnp.mean(x * x, axis=-1, keepdims=True)
    h = x * lax.rsqrt(ms + EPS) * g
    return h * (1.0 + mod[1:2]) + mod[0:1]


def _gated_residual(x, y, mod, g):
    ms = jnp.mean(y * y, axis=-1, keepdims=True)
    return x + mod[2:3] * (y * lax.rsqrt(ms + EPS) * g)


def _adaln_kernel(c_ref, w_ref, b_ref, o_ref):
    sc = _silu(c_ref[...]).astype(BF16)
    o_ref[0] = jnp.dot(sc, w_ref[0].astype(BF16), preferred_element_type=F32) + b_ref[0]


def _adaln(c_all, ada_w, ada_b):
    depth, d, d3 = ada_w.shape
    n = c_all.shape[0]
    nj = d3 // d
    return pl.pallas_call(
        _adaln_kernel,
        grid=(depth, nj),
        in_specs=[
            pl.BlockSpec((n, d), lambda i, j: (0, 0)),
            pl.BlockSpec((1, d, d), lambda i, j: (i, 0, j)),
            pl.BlockSpec((1, 1, d), lambda i, j: (i, 0, j)),
        ],
        out_specs=pl.BlockSpec((1, n, d), lambda i, j: (i, 0, j)),
        out_shape=jax.ShapeDtypeStruct((depth, n, d3), F32),
        compiler_params=_params(2),
        name="adaln",
    )(c_all, ada_w, ada_b.reshape(depth, 1, d3))


def _lru_kernel(x_ref, mod_ref, gpre_ref, gpost_ref, win_ref, cw_ref, cb_ref, wax_ref,
                ba_ref, bx_ref, lam_ref, wout_ref, conv0_ref, h0_ref,
                y_ref, convo_ref, hlast_ref,
                xbuf, xc_s, pre_a, pre_x, gate_s, yv_s, hcar, *, tt, r, nblk):
    t = pl.program_id(1)
    nt = pl.num_programs(1)
    tail0 = SUBLANES - (CONV_TAPS - 1)

    @pl.when(t == 0)
    def _():
        xbuf[tail0:SUBLANES, :] = conv0_ref[0]
        hcar[...] = h0_ref[0]

    x = x_ref[0]
    mod = mod_ref[0]
    h = _modulated_norm(x, mod, gpre_ref[...])
    z = jnp.dot(h.astype(BF16), win_ref[...], preferred_element_type=F32)
    xbuf[SUBLANES:SUBLANES + tt, :] = z[:, :r]
    gate_s[...] = z[:, r:]

    cw = cw_ref[...]
    xc = cb_ref[...] + xbuf[tail0:tail0 + tt, :] * cw[0:1]
    for k in range(1, CONV_TAPS):
        xc = xc + xbuf[tail0 + k:tail0 + k + tt, :] * cw[k:k + 1]
    tail = xbuf[tt + tail0:tt + SUBLANES, :]
    xbuf[tail0:SUBLANES, :] = tail
    xc_s[...] = xc

    @pl.when(t == nt - 1)
    def _():
        convo_ref[0] = tail

    xcb = xc.astype(BF16)
    for n in range(nblk):
        lo, hi = n * LANES, (n + 1) * LANES
        res = jnp.dot(xcb[:, lo:hi], wax_ref[n], preferred_element_type=F32)
        pre_a[:, lo:hi] = res[:, :LANES]
        pre_x[:, lo:hi] = res[:, LANES:]

    coef = -LRU_C * _softplus(-lam_ref[...])
    coef2n = -2.0 * coef
    ba = ba_ref[...]
    bx = bx_ref[...]
    row = lax.broadcasted_iota(jnp.int32, (SUBLANES, r), 0)

    def group(g, hprev):
        st = pl.multiple_of(g * SUBLANES, SUBLANES)
        sl = pl.ds(st, SUBLANES)
        rg = _sigmoid(pre_a[sl, :] + ba)
        ig = _sigmoid(pre_x[sl, :] + bx)
        a = jnp.exp(coef * rg)
        u = _sqrt_pos(_one_minus_exp_neg(coef2n * rg, a * a)) * (ig * xc_s[sl, :])
        for s in (1, 2, 4):
            a_sh = jnp.where(row >= s, pltpu.roll(a, s, axis=0), 1.0)
            u_sh = jnp.where(row >= s, pltpu.roll(u, s, axis=0), 0.0)
            u = a * u_sh + u
            a = a * a_sh
        hs = a * hprev + u
        yv_s[sl, :] = hs * _silu(gate_s[sl, :])
        return hs[SUBLANES - 1:SUBLANES, :]

    hlast = lax.fori_loop(0, tt // SUBLANES, group, hcar[...])
    hcar[...] = hlast

    @pl.when(t == nt - 1)
    def _():
        hlast_ref[0] = hlast

    y = jnp.dot(yv_s[...].astype(BF16), wout_ref[...], preferred_element_type=F32)
    y_ref[0] = _gated_residual(x, y, mod, gpost_ref[...])


def _lru_layer(x, mod, mod_off, gpre, gpost, w_in, conv_w, conv_b, w_ax, b_a, b_x, lam, w_out,
               conv0, h0, *, tt):
    b, t, d = x.shape
    r = lam.shape[-1]
    nblk = w_ax.shape[0]
    kern = functools.partial(_lru_kernel, tt=tt, r=r, nblk=nblk)
    return pl.pallas_call(
        kern,
        grid=(b, t // tt),
        in_specs=[
            pl.BlockSpec((1, tt, d), lambda i, j: (i, j, 0)),
            pl.BlockSpec((1, 3, d), lambda i, j: (i + mod_off, 0, 0)),
            _const_spec((1, d)), _const_spec((1, d)),
            _const_spec((d, 2 * r)),
            _const_spec((CONV_TAPS, r)), _const_spec((1, r)),
            _const_spec((nblk, LANES, 2 * LANES)),
            _const_spec((1, r)), _const_spec((1, r)), _const_spec((1, r)),
            _const_spec((r, d)),
            pl.BlockSpec((1, CONV_TAPS - 1, r), lambda i, j: (i, 0, 0)),
            pl.BlockSpec((1, 1, r), lambda i, j: (i, 0, 0)),
        ],
        out_specs=[
            pl.BlockSpec((1, tt, d), lambda i, j: (i, j, 0)),
            pl.BlockSpec((1, CONV_TAPS - 1, r), lambda i, j: (i, 0, 0)),
            pl.BlockSpec((1, 1, r), lambda i, j: (i, 0, 0)),
        ],
        out_shape=[
            jax.ShapeDtypeStruct((b, t, d), F32),
            jax.ShapeDtypeStruct((b, CONV_TAPS - 1, r), F32),
            jax.ShapeDtypeStruct((b, 1, r), F32),
        ],
        scratch_shapes=[
            pltpu.VMEM((tt + SUBLANES, r), F32),
            pltpu.VMEM((tt, r), F32), pltpu.VMEM((tt, r), F32), pltpu.VMEM((tt, r), F32),
            pltpu.VMEM((tt, r), F32), pltpu.VMEM((tt, r), F32),
            pltpu.VMEM((1, r), F32),
        ],
        compiler_params=_params(2),
        name="lru_layer",
    )(x, mod, gpre, gpost, w_in, conv_w, conv_b, w_ax, b_a, b_x, lam, w_out, conv0, h0)


def _split3(x):
    x1 = x.astype(BF16)
    r1 = x - x1.astype(F32)
    x2 = r1.astype(BF16)
    x3 = (r1 - x2.astype(F32)).astype(BF16)
    return x1, x2, x3


def _tri_cumsum(tri, x, carry):
    c = carry
    for piece in _split3(x):
        c = c + jnp.dot(tri, piece, preferred_element_type=F32)
    return c


def _fox_in_kernel(x_ref, mod_ref, gpre_ref, w_ref, wvt_ref, wf_ref, bf_ref, tri_ref, hsel_ref,
                   q_ref, kb_ref, k_ref, v_ref, sg_ref, vt_ref, lf_ref, cum_ref, aug_ref,
                   qn_ref, kn_ref, car, *, w, nh, qscale):
    @pl.when(pl.program_id(1) == 0)
    def _():
        car[...] = jnp.zeros_like(car)

    hb = _modulated_norm(x_ref[0], mod_ref[0], gpre_ref[...]).astype(BF16)
    z = jnp.dot(hb, w_ref[...], preferred_element_type=F32)
    q = z[:, :w] * qscale
    q_ref[0] = q.astype(BF16)
    k = z[:, w:2 * w]
    k_ref[0] = k
    kb_ref[0] = k.astype(BF16)
    hsel = hsel_ref[...]
    qn_ref[0, 0] = jnp.max(jnp.dot((q * q).astype(BF16), hsel, preferred_element_type=F32),
                           axis=0, keepdims=True)
    kn_ref[0, 0] = jnp.max(jnp.dot((k * k).astype(BF16), hsel, preferred_element_type=F32),
                           axis=0, keepdims=True)
    v_ref[0] = z[:, 2 * w:3 * w]
    sg_ref[0] = _silu(z[:, 3 * w:]).astype(BF16)
    vt = lax.dot_general(wvt_ref[...], hb, _NT, preferred_element_type=F32)
    tt = vt.shape[1]
    vt = vt.reshape(nh, HEAD_DIM, tt)
    vt_ref[0, :, 0] = jnp.concatenate([vt, jnp.ones((nh, ONES_ROWS, tt), F32)], axis=1).astype(BF16)
    fl = jnp.dot(hb, wf_ref[...], preferred_element_type=F32) + bf_ref[...]
    lf = -_softplus(-fl)
    c = _tri_cumsum(tri_ref[...], lf, car[...])
    tt = c.shape[0]
    car[...] = c[tt - 1:tt, :]
    lf_ref[0] = lf[:, :nh]
    cum_ref[0] = c[:, :nh]
    y1, y2, y3 = _split3(c * (-LOG2E))
    lane = lax.broadcasted_iota(jnp.int32, c.shape, 1)
    aug = jnp.where(lane < nh, y1.astype(F32),
                    jnp.where(lane < 2 * nh, y2.astype(F32),
                              jnp.where(lane < 3 * nh, y3.astype(F32), 0.0)))
    aug_ref[0] = aug.astype(BF16)


def _tri(n):
    return jnp.tril(jnp.ones((n, n), F32)).astype(BF16)


def _fox_in(x, mod, mod_off, gpre, w_main, w_vt, w_f3, b_f3, nh, *, tt, qscale):
    b, t, d = x.shape
    w = w_main.shape[1] // 4
    rows = HEAD_DIM + ONES_ROWS
    kern = functools.partial(_fox_in_kernel, w=w, nh=nh, qscale=qscale)
    tile = lambda dt: jax.ShapeDtypeStruct((b, t, w), dt)
    tile_spec = pl.BlockSpec((1, tt, w), lambda i, j: (i, j, 0))
    head_spec = pl.BlockSpec((1, tt, nh), lambda i, j: (i, j, 0))
    norm_spec = pl.BlockSpec((1, 1, 1, LANES), lambda i, j: (i, j, 0, 0))
    hsel = (jnp.arange(w)[:, None] // HEAD_DIM == jnp.arange(LANES)[None, :]).astype(BF16)
    return pl.pallas_call(
        kern,
        grid=(b, t // tt),
        in_specs=[
            pl.BlockSpec((1, tt, d), lambda i, j: (i, j, 0)),
            pl.BlockSpec((1, 3, d), lambda i, j: (i + mod_off, 0, 0)),
            _const_spec((1, d)),
            _const_spec((d, 4 * w)),
            _const_spec((w, d)),
            _const_spec((d, LANES)),
            _const_spec((1, LANES)),
            _const_spec((tt, tt)),
            _const_spec((w, LANES)),
        ],
        out_specs=[tile_spec] * 5 + [pl.BlockSpec((1, nh, 1, rows, tt), lambda i, j: (i, 0, j, 0, 0)),
                                     head_spec, head_spec,
                                     pl.BlockSpec((1, tt, LANES), lambda i, j: (i, j, 0)),
                                     norm_spec, norm_spec],
        out_shape=[tile(BF16), tile(BF16), tile(F32), tile(F32), tile(BF16),
                   jax.ShapeDtypeStruct((b, nh, t // tt, rows, tt), BF16),
                   jax.ShapeDtypeStruct((b, t, nh), F32), jax.ShapeDtypeStruct((b, t, nh), F32),
                   jax.ShapeDtypeStruct((b, t, LANES), BF16),
                   jax.ShapeDtypeStruct((b, t // tt, 1, LANES), F32),
                   jax.ShapeDtypeStruct((b, t // tt, 1, LANES), F32)],
        scratch_shapes=[pltpu.VMEM((1, LANES), F32)],
        compiler_params=_params(2),
        name="fox_in",
    )(x, mod, gpre, w_main, w_vt, w_f3, b_f3, _tri(tt), hsel)


def _cumsum_kernel(x_ref, tri_ref, o_ref, car):
    @pl.when(pl.program_id(1) == 0)
    def _():
        car[...] = jnp.zeros_like(car)

    c = _tri_cumsum(tri_ref[...], x_ref[0], car[...])
    o_ref[0] = c
    tc = c.shape[0]
    car[...] = c[tc - 1:tc, :]


def _cumsum_time(x, *, tc):
    b, t, nh = x.shape
    return pl.pallas_call(
        _cumsum_kernel,
        grid=(b, t // tc),
        in_specs=[pl.BlockSpec((1, tc, nh), lambda i, j: (i, j, 0)), _const_spec((tc, tc))],
        out_specs=pl.BlockSpec((1, tc, nh), lambda i, j: (i, j, 0)),
        out_shape=jax.ShapeDtypeStruct((b, t, nh), F32),
        scratch_shapes=[pltpu.VMEM((1, nh), F32)],
        compiler_params=_params(2),
        name="cumsum_time",
    )(x, _tri(tc))


def _split_heads(q):
    qf = q.astype(F32)
    lane = lax.broadcasted_iota(jnp.int32, qf.shape, 1)
    return (jnp.where(lane < HEAD_DIM, qf, 0.0).astype(BF16),
            jnp.where(lane >= HEAD_DIM, qf, 0.0).astype(BF16))


def _attn_kernel(cnt_ref, q_ref, k_ref, a_ref, vt_ref, o_ref, z0, z1, zm0, zm1, p0, p1, al0, al1,
                 qa_s, m_s, acc_s, *, tq, tk, nh):
    bi = pl.program_id(0)
    hp = pl.program_id(1)
    qi = pl.program_id(2)
    nq = pl.num_programs(2)
    z_s, zmax_s, p_s, alpha_s = (z0, z1), (zm0, zm1), (p0, p1), (al0, al1)

    @pl.when((bi == 0) & (hp == 0) & (qi == 0))
    def _():
        for p_ref, al_ref in zip(p_s, alpha_s):
            p_ref[...] = jnp.zeros_like(p_ref)
            al_ref[...] = jnp.ones_like(al_ref)

    q = q_ref[0].astype(F32)
    lane = lax.broadcasted_iota(jnp.int32, (tq, LANES), 1)
    for hh in range(2):
        head = 2 * hp + hh
        own = (lane >= hh * HEAD_DIM) & (lane < (hh + 1) * HEAD_DIM)
        pick = (lane == head) | (lane == head + nh) | (lane == head + 2 * nh)
        qa_s[hh] = jnp.concatenate(
            [jnp.where(own, q, 0.0).astype(BF16), jnp.where(pick, 1.0, 0.0).astype(BF16)], axis=1)
    m_s[...] = jnp.full_like(m_s, NEG_BIG)
    acc_s[...] = jnp.zeros_like(acc_s)
    tv = vt_ref.shape[4]
    nsub = tk // tv

    n0 = cnt_ref[(bi * nh + 2 * hp) * nq + qi]
    n_off = n0 + cnt_ref[(bi * nh + 2 * hp + 1) * nq + qi]
    last = n_off + 1

    def item(t):
        u = jnp.maximum(t, 0) - 2
        hh = jnp.where(u < 0, u + 2, (u >= n0).astype(jnp.int32))
        off = jnp.where(u < 0, 0, u - hh * n0 + 1)
        return hh, qi - off

    def qk(t, slot, diagonal_head=None):
        if diagonal_head is None:
            hh, blk = item(t)
        else:
            hh, blk = diagonal_head, qi
        st = pl.multiple_of(blk * tk, tk)
        keys = jnp.concatenate([k_ref[0, pl.ds(st, tk), :], a_ref[0, pl.ds(st, tk), :]], axis=1)
        z = lax.dot_general(keys, qa_s[hh], _NT, preferred_element_type=F32)
        if diagonal_head is not None:
            kpos = lax.broadcasted_iota(jnp.int32, z.shape, 0)
            qpos = lax.broadcasted_iota(jnp.int32, z.shape, 1)
            z = jnp.where(kpos <= qpos, z, NEG_BIG)
        z_s[slot][...] = z
        zmax_s[slot][...] = jnp.max(z, axis=0, keepdims=True)

    def softmax(t, slot):
        hh, _ = item(t)
        m_prev = m_s[hh]
        m_next = jnp.maximum(m_prev, zmax_s[slot][...])
        p_s[slot][...] = jnp.exp2(z_s[slot][...] - m_next).astype(BF16)
        alpha_s[slot][...] = jnp.exp2(m_prev - m_next)
        m_s[hh] = m_next

    def pv(t, slot, valid):
        hh, blk = item(t)
        r = None
        for u in range(nsub):
            part = jnp.dot(vt_ref[0, hh, blk * nsub + u], p_s[slot][u * tv:(u + 1) * tv, :],
                           preferred_element_type=F32)
            r = part if r is None else r + part
        if valid is not None:
            r = jnp.where(valid, r, 0.0)
        acc_s[hh] = acc_s[hh] * alpha_s[slot][...] + r

    def iteration(t, slot, next_diagonal_head=None):
        pv(t - 2, slot, t >= 2)
        softmax(t, slot)
        qk(t + 1, 1 - slot, next_diagonal_head)

    def finish(slot):
        pv(last - 2, slot, last >= 2)
        softmax(last, slot)
        pv(last - 1, 1 - slot, None)
        pv(last, slot, None)
        outs = []
        for hh in range(2):
            acc = acc_s[hh]
            outs.append(acc[:HEAD_DIM] / acc[HEAD_DIM:HEAD_DIM + 1])
        o_ref[0] = jnp.concatenate(outs, axis=0).T.astype(o_ref.dtype)

    qk(0, 0, 0)
    iteration(0, 0, 1)

    def pair(k, carry):
        iteration(2 * k + 1, 1)
        iteration(2 * k + 2, 0)
        return carry

    lax.fori_loop(0, n_off // 2, pair, 0)
    odd = lax.rem(n_off, 2) == 1

    @pl.when(odd)
    def _():
        iteration(last - 1, 1)
        finish(0)

    @pl.when(jnp.logical_not(odd))
    def _():
        finish(1)


def _attention(cnt, qb, kb, aug, vt, nh, *, tq, tk):
    assert tq == tk, "one key block per query tile on the diagonal"
    b, t, w = qb.shape
    hp = w // LANES
    nq = t // tq
    _, _, nv, rows, tv = vt.shape
    kern = functools.partial(_attn_kernel, tq=tq, tk=tk, nh=nh)
    grid_spec = pltpu.PrefetchScalarGridSpec(
        num_scalar_prefetch=1,
        grid=(b, hp, nq),
        in_specs=[
            pl.BlockSpec((1, tq, LANES), lambda i, h, j, c: (i, j, h)),
            pl.BlockSpec((1, t, LANES), lambda i, h, j, c: (i, 0, h)),
            pl.BlockSpec((1, t, LANES), lambda i, h, j, c: (i, 0, 0)),
            pl.BlockSpec((1, 2, nv, rows, tv), lambda i, h, j, c: (i, h, 0, 0, 0)),
        ],
        out_specs=pl.BlockSpec((1, tq, LANES), lambda i, h, j, c: (i, j, h)),
        scratch_shapes=(
            [pltpu.VMEM((tk, tq), F32)] * 2
            + [pltpu.VMEM((1, tq), F32)] * 2
            + [pltpu.VMEM((tk, tq), BF16)] * 2
            + [pltpu.VMEM((1, tq), F32)] * 2
            + [pltpu.VMEM((2, tq, 2 * LANES), BF16),
               pltpu.VMEM((2, 1, tq), F32),
               pltpu.VMEM((2, rows, tq), F32)]))
    return pl.pallas_call(
        kern,
        grid_spec=grid_spec,
        out_shape=jax.ShapeDtypeStruct((b, t, w), BF16),
        compiler_params=_params(3),
        name="fox_attention",
    )(cnt, qb, kb, aug, vt)


EXP2_ZERO_BELOW = 150.0


def _plan_kernel(qn_ref, kn_ref, cend_ref, cnt_ref, *, nh, ndiag):
    qn2 = jnp.max(qn_ref[0], axis=1)[:, :nh]
    kn2 = jnp.max(kn_ref[0], axis=0, keepdims=True)[:, :nh]
    bend = cend_ref[0] * (-LOG2E)
    nk = bend.shape[0]
    reach = 2.04 * jnp.sqrt(qn2 * kn2)
    blk = lax.broadcasted_iota(jnp.int32, (nk, nh), 0)
    rows = []
    for qi in range(qn2.shape[0]):
        d0 = qi * ndiag
        bprev = bend[d0 - 1:d0] if d0 > 0 else jnp.zeros((1, nh), F32)
        slack = 2.0 + 1e-5 * (jnp.abs(bprev) + jnp.abs(bend))
        live = (blk < d0) & (bprev - bend <= EXP2_ZERO_BELOW + reach[qi:qi + 1] + slack)
        rows.append(jnp.max(jnp.where(live, d0 - blk, 0), axis=0, keepdims=True))
    cnt_ref[0] = jnp.concatenate(rows, axis=0)


def _plan(qn2, kn2, cend, nh):
    b, nq, sub, _ = qn2.shape
    nt = kn2.shape[1]
    nk = cend.shape[1]
    return pl.pallas_call(
        functools.partial(_plan_kernel, nh=nh, ndiag=nk // nq),
        grid=(b,),
        in_specs=[
            pl.BlockSpec((1, nq, sub, LANES), lambda i: (i, 0, 0, 0)),
            pl.BlockSpec((1, nt, LANES), lambda i: (i, 0, 0)),
            pl.BlockSpec((1, nk, nh), lambda i: (i, 0, 0)),
        ],
        out_specs=pl.BlockSpec((1, nq, nh), lambda i: (i, 0, 0)),
        out_shape=jax.ShapeDtypeStruct((b, nq, nh), jnp.int32),
        compiler_params=_params(1),
        name="fox_plan",
    )(qn2, kn2, cend)


def _attn_cached_kernel(q_ref, kn_ref, vn_ref, pk_ref, pv_ref, cp_ref, cn_ref, o_ref):
    qh = _split_heads(q_ref[0])
    pkb = pk_ref[0].astype(BF16)
    pvb = pv_ref[0].astype(BF16)
    kn = kn_ref[0]
    vn = vn_ref[0]
    t = kn.shape[0]
    p_len = pkb.shape[0]
    rowi = lax.broadcasted_iota(jnp.int32, (t, t), 0)
    coli = lax.broadcasted_iota(jnp.int32, (t, t), 1)
    outs = []
    for hh in range(2):
        cp = cp_ref[0, 0, hh:hh + 1, :]
        tot = cp[:, p_len - 1:p_len]
        z1 = lax.dot_general(qh[hh], pkb, _NT, preferred_element_type=F32) - (cp - tot) * LOG2E
        z2 = (lax.dot_general(qh[hh], kn, _NT, preferred_element_type=F32)
              - cn_ref[0, 0, hh:hh + 1, :] * LOG2E)
        z2 = jnp.where(coli <= rowi, z2, NEG_BIG)
        m = jnp.maximum(jnp.max(z1, axis=1), jnp.max(z2, axis=1))[:, None]
        p1 = jnp.exp2(z1 - m)
        p2 = jnp.exp2(z2 - m)
        l = (jnp.sum(p1, axis=1) + jnp.sum(p2, axis=1))[:, None]
        o = (jnp.dot(p1.astype(BF16), pvb, preferred_element_type=F32)
             + jnp.dot(p2.astype(BF16), vn, preferred_element_type=F32))
        outs.append(o / l)
    lane = lax.broadcasted_iota(jnp.int32, (t, LANES), 1)
    o_ref[0] = jnp.where(lane < HEAD_DIM, outs[0], outs[1]).astype(o_ref.dtype)


def _attention_cached(qb, kb, vb, past_k, past_v, cum_past, cum_new):
    b, t, w = qb.shape
    p_len = past_k.shape[1]
    hp = w // LANES
    new_spec = pl.BlockSpec((1, t, LANES), lambda i, h: (i, 0, h))
    past_spec = pl.BlockSpec((1, p_len, LANES), lambda i, h: (i, 0, h))
    return pl.pallas_call(
        _attn_cached_kernel,
        grid=(b, hp),
        in_specs=[new_spec, new_spec, new_spec, past_spec, past_spec,
                  pl.BlockSpec((1, 1, 2, p_len), lambda i, h: (i, h, 0, 0)),
                  pl.BlockSpec((1, 1, 2, t), lambda i, h: (i, h, 0, 0))],
        out_specs=new_spec,
        out_shape=jax.ShapeDtypeStruct((b, t, w), BF16),
        compiler_params=_params(2),
        name="fox_attention_cached",
    )(qb, kb, vb, past_k, past_v,
      cum_past.reshape(b, hp, 2, p_len), cum_new.reshape(b, hp, 2, t))


def _fox_out_kernel(o_ref, sg_ref, x_ref, mod_ref, gpost_ref, w_ref, y_ref):
    yv = (o_ref[0].astype(F32) * sg_ref[0].astype(F32)).astype(BF16)
    y = jnp.dot(yv, w_ref[...], preferred_element_type=F32)
    y_ref[0] = _gated_residual(x_ref[0], y, mod_ref[0], gpost_ref[...])


def _fox_out(o, g, x, mod, mod_off, gpost, w_out, *, tt):
    b, t, d = x.shape
    w = o.shape[-1]
    return pl.pallas_call(
        _fox_out_kernel,
        grid=(b, t // tt),
        in_specs=[
            pl.BlockSpec((1, tt, w), lambda i, j: (i, j, 0)),
            pl.BlockSpec((1, tt, w), lambda i, j: (i, j, 0)),
            pl.BlockSpec((1, tt, d), lambda i, j: (i, j, 0)),
            pl.BlockSpec((1, 3, d), lambda i, j: (i + mod_off, 0, 0)),
            _const_spec((1, d)),
            _const_spec((w, d)),
        ],
        out_specs=pl.BlockSpec((1, tt, d), lambda i, j: (i, j, 0)),
        out_shape=jax.ShapeDtypeStruct((b, t, d), F32),
        compiler_params=_params(2),
        name="fox_out",
    )(o, g, x, mod, gpost, w_out)


def _tile(t, pref):
    return pref if t % pref == 0 else t


def kernel(x_prompt, x_sample, c_prompt, c_sample, state_lru_h, state_lru_conv, cache_fox_k, cache_fox_v, cache_fox_logf, norm_pre, norm_post, ada_w, ada_b, lru_w_in, lru_conv_w, lru_conv_b, lru_w_a, lru_b_a, lru_w_x, lru_b_x, lru_lambda, lru_w_out, fox_w_in, fox_b_f, fox_w_out):
    b, t, d = x_prompt.shape
    db, dt, _ = x_sample.shape
    depth = ada_w.shape[0]
    r = lru_lambda.shape[-1]
    nh = fox_b_f.shape[-1]
    w = fox_w_out.shape[1]
    p_len = cache_fox_k.shape[2]
    scale = float(HEAD_DIM) ** -0.5

    c_all = jnp.concatenate([c_prompt, c_sample], axis=0)
    mod = _adaln(c_all, ada_w, ada_b).reshape(depth, b + db, 3, d)

    xp, xs = x_prompt, x_sample

    j = 0
    w_ax = jnp.concatenate([lru_w_a[j], lru_w_x[j]], axis=-1).astype(BF16)
    lru_args = (norm_pre[0:1], norm_post[0:1], lru_w_in[j].astype(BF16), lru_conv_w[j],
                lru_conv_b[j:j + 1], w_ax, lru_b_a[j:j + 1], lru_b_x[j:j + 1],
                lru_lambda[j:j + 1], lru_w_out[j].astype(BF16))
    xp, conv_p, h_p = _lru_layer(
        xp, mod[0], 0, *lru_args,
        jnp.zeros((b, CONV_TAPS - 1, r), F32), jnp.zeros((b, 1, r), F32), tt=_tile(t, 512))
    xs, conv_s, h_s = _lru_layer(
        xs, mod[0], b, *lru_args,
        state_lru_conv[j], state_lru_h[j].reshape(db, 1, r), tt=_tile(dt, 256))

    w_main = fox_w_in[j][:, :4 * w].astype(BF16)
    pad = jnp.zeros((d, LANES - 3 * nh), F32)
    w_f3 = jnp.concatenate([fox_w_in[j][:, 4 * w:]] * 3 + [pad], axis=1).astype(BF16)
    b_f3 = jnp.concatenate([fox_b_f[j]] * 3 + [jnp.zeros((LANES - 3 * nh,), F32)]).reshape(1, LANES)
    w_out = fox_w_out[j].astype(BF16)
    qscale = scale * LOG2E

    w_vt = fox_w_in[j][:, 2 * w:3 * w].T.astype(BF16)
    tt = _tile(t, 256)
    tk = _tile(t, 512)
    tq = tk
    qb, kb, k_p, v_p, sg_p, vt_p, lf_p, cum_p, aug_p, qn_p, kn_p = _fox_in(
        xp, mod[1], 0, norm_pre[1:2], w_main, w_vt, w_f3, b_f3, nh, tt=tt, qscale=qscale)
    cnt = _plan(qn_p.reshape(b, t // tq, tq // tt, LANES), kn_p.reshape(b, t // tt, LANES),
                cum_p[:, tk - 1::tk, :], nh)
    o_p = _attention(cnt.transpose(0, 2, 1).reshape(-1), qb, kb, aug_p, vt_p, nh, tq=tq, tk=tk)
    yp = _fox_out(o_p, sg_p, xp, mod[1], 0, norm_post[1:2], w_out, tt=_tile(t, 512))

    qs, ksb, k_s, v_s, sg_s, _, lf_s, cum_s, _, _, _ = _fox_in(
        xs, mod[1], b, norm_pre[1:2], w_main, w_vt, w_f3, b_f3, nh, tt=_tile(dt, 256), qscale=qscale)
    lf_past = cache_fox_logf[j].transpose(1, 0, 2).reshape(1, p_len, db * nh)
    cum_past = _cumsum_time(lf_past, tc=_tile(p_len, 512)).reshape(p_len, db, nh).transpose(1, 2, 0)
    o_s = _attention_cached(qs, ksb, v_s.astype(BF16),
                            cache_fox_k[j].reshape(db, p_len, w),
                            cache_fox_v[j].reshape(db, p_len, w),
                            cum_past, jnp.swapaxes(cum_s, 1, 2))
    ys = _fox_out(o_s, sg_s, xs, mod[1], b, norm_post[1:2], w_out, tt=_tile(dt, 512))

    hd = (nh, HEAD_DIM)
    return (yp, ys,
            h_p.reshape(1, b, r), conv_p[None],
            k_p.reshape(1, b, t, *hd), v_p.reshape(1, b, t, *hd), lf_p[None],
            h_s.reshape(1, db, r), conv_s[None],
            k_s.reshape(1, db, dt, *hd), v_s.reshape(1, db, dt, *hd), lf_s[None])
```

```python
import functools

import jax
import jax.numpy as jnp
from jax import lax
from jax.experimental import pallas as pl
from jax.experimental.pallas import tpu as pltpu

F32 = jnp.float32
BF16 = jnp.bfloat16

EPS = 1e-6
LRU_C = 8.0
CONV_TAPS = 4
HEAD_DIM = 64
LANES = 128
SUBLANES = 8
NEG_BIG = -1e30
LOG2E = 1.4426950408889634
ONES_ROWS = 16
VMEM_LIMIT_BYTES = 56 * 1024 * 1024

_NT = (((1,), (1,)), ((), ()))


def _params(n_grid):
    return pltpu.CompilerParams(
        dimension_semantics=("arbitrary",) * n_grid,
        vmem_limit_bytes=VMEM_LIMIT_BYTES)


def _const_spec(shape):
    nd = len(shape)
    return pl.BlockSpec(shape, lambda *_: (0,) * nd, pipeline_mode=pl.Buffered(1))


def _sigmoid(x):
    return 0.5 * jnp.tanh(0.5 * x) + 0.5


def _silu(x):
    return x * _sigmoid(x)


def _softplus(x):
    return jnp.maximum(x, 0.0) + jnp.log1p(jnp.exp(-jnp.abs(x)))


def _one_minus_exp_neg(t, exp_neg_t):
    p = jnp.full_like(t, -1.0 / 24.0)
    for c in (1.0 / 6.0, -0.5, 1.0):
        p = p * t + c
    return jnp.where(t < 1.0 / 64.0, t * p, 1.0 - exp_neg_t)


def _sqrt_pos(x):
    x = jnp.maximum(x, 1e-30)
    return x * lax.rsqrt(x)


def _modulated_norm(x, mod, g):
    ms = jnp.mean(x * x, axis=-1, keepdims=True)
    h = x * lax.rsqrt(ms + EPS) * g
    return h * (1.0 + mod[1:2]) + mod[0:1]


def _gated_residual(x, y, mod, g):
    ms = jnp.mean(y * y, axis=-1, keepdims=True)
    return x + mod[2:3] * (y * lax.rsqrt(ms + EPS) * g)


def _adaln_kernel(c_ref, w_ref, b_ref, o_ref):
    sc = _silu(c_ref[...]).astype(BF16)
    o_ref[0] = jnp.dot(sc, w_ref[0].astype(BF16), preferred_element_type=F32) + b_ref[0]


def _adaln(c_all, ada_w, ada_b):
    depth, d, d3 = ada_w.shape
    n = c_all.shape[0]
    nj = d3 // d
    return pl.pallas_call(
        _adaln_kernel,
        grid=(depth, nj),
        in_specs=[
            pl.BlockSpec((n, d), lambda i, j: (0, 0)),
            pl.BlockSpec((1, d, d), lambda i, j: (i, 0, j)),
            pl.BlockSpec((1, 1, d), lambda i, j: (i, 0, j)),
        ],
        out_specs=pl.BlockSpec((1, n, d), lambda i, j: (i, 0, j)),
        out_shape=jax.ShapeDtypeStruct((depth, n, d3), F32),
        compiler_params=_params(2),
        name="adaln",
    )(c_all, ada_w, ada_b.reshape(depth, 1, d3))


def _lru_kernel(x_ref, mod_ref, gpre_ref, gpost_ref, win_ref, cw_ref, cb_ref, wax_ref,
                ba_ref, bx_ref, lam_ref, wout_ref, conv0_ref, h0_ref,
                y_ref, convo_ref, hlast_ref,
                xbuf, xc_s, pre_a, pre_x, gate_s, yv_s, hcar, *, tt, r, nblk):
    t = pl.program_id(1)
    nt = pl.num_programs(1)
    tail0 = SUBLANES - (CONV_TAPS - 1)

    @pl.when(t == 0)
    def _():
        xbuf[tail0:SUBLANES, :] = conv0_ref[0]
        hcar[...] = h0_ref[0]

    x = x_ref[0]
    mod = mod_ref[0]
    h = _modulated_norm(x, mod, gpre_ref[...])
    z = jnp.dot(h.astype(BF16), win_ref[...], preferred_element_type=F32)
    xbuf[SUBLANES:SUBLANES + tt, :] = z[:, :r]
    gate_s[...] = z[:, r:]

    cw = cw_ref[...]
    xc = cb_ref[...] + xbuf[tail0:tail0 + tt, :] * cw[0:1]
    for k in range(1, CONV_TAPS):
        xc = xc + xbuf[tail0 + k:tail0 + k + tt, :] * cw[k:k + 1]
    tail = xbuf[tt + tail0:tt + SUBLANES, :]
    xbuf[tail0:SUBLANES, :] = tail
    xc_s[...] = xc

    @pl.when(t == nt - 1)
    def _():
        convo_ref[0] = tail

    xcb = xc.astype(BF16)
    for n in range(nblk):
        lo, hi = n * LANES, (n + 1) * LANES
        res = jnp.dot(xcb[:, lo:hi], wax_ref[n], preferred_element_type=F32)
        pre_a[:, lo:hi] = res[:, :LANES]
        pre_x[:, lo:hi] = res[:, LANES:]

    coef = -LRU_C * _softplus(-lam_ref[...])
    coef2n = -2.0 * coef
    ba = ba_ref[...]
    bx = bx_ref[...]
    row = lax.broadcasted_iota(jnp.int32, (SUBLANES, r), 0)

    def group(g, hprev):
        st = pl.multiple_of(g * SUBLANES, SUBLANES)
        sl = pl.ds(st, SUBLANES)
        rg = _sigmoid(pre_a[sl, :] + ba)
        ig = _sigmoid(pre_x[sl, :] + bx)
        a = jnp.exp(coef * rg)
        u = _sqrt_pos(_one_minus_exp_neg(coef2n * rg, a * a)) * (ig * xc_s[sl, :])
        for s in (1, 2, 4):
            a_sh = jnp.where(row >= s, pltpu.roll(a, s, axis=0), 1.0)
            u_sh = jnp.where(row >= s, pltpu.roll(u, s, axis=0), 0.0)
            u = a * u_sh + u
            a = a * a_sh
        hs = a * hprev + u
        yv_s[sl, :] = hs * _silu(gate_s[sl, :])
        return hs[SUBLANES - 1:SUBLANES, :]

    hlast = lax.fori_loop(0, tt // SUBLANES, group, hcar[...])
    hcar[...] = hlast

    @pl.when(t == nt - 1)
    def _():
        hlast_ref[0] = hlast

    y = jnp.dot(yv_s[...].astype(BF16), wout_ref[...], preferred_element_type=F32)
    y_ref[0] = _gated_residual(x, y, mod, gpost_ref[...])


def _lru_layer(x, mod, mod_off, gpre, gpost, w_in, conv_w, conv_b, w_ax, b_a, b_x, lam, w_out,
               conv0, h0, *, tt):
    b, t, d = x.shape
    r = lam.shape[-1]
    nblk = w_ax.shape[0]
    kern = functools.partial(_lru_kernel, tt=tt, r=r, nblk=nblk)
    return pl.pallas_call(
        kern,
        grid=(b, t // tt),
        in_specs=[
            pl.BlockSpec((1, tt, d), lambda i, j: (i, j, 0)),
            pl.BlockSpec((1, 3, d), lambda i, j: (i + mod_off, 0, 0)),
            _const_spec((1, d)), _const_spec((1, d)),
            _const_spec((d, 2 * r)),
            _const_spec((CONV_TAPS, r)), _const_spec((1, r)),
            _const_spec((nblk, LANES, 2 * LANES)),
            _const_spec((1, r)), _const_spec((1, r)), _const_spec((1, r)),
            _const_spec((r, d)),
            pl.BlockSpec((1, CONV_TAPS - 1, r), lambda i, j: (i, 0, 0)),
            pl.BlockSpec((1, 1, r), lambda i, j: (i, 0, 0)),
        ],
        out_specs=[
            pl.BlockSpec((1, tt, d), lambda i, j: (i, j, 0)),
            pl.BlockSpec((1, CONV_TAPS - 1, r), lambda i, j: (i, 0, 0)),
            pl.BlockSpec((1, 1, r), lambda i, j: (i, 0, 0)),
        ],
        out_shape=[
            jax.ShapeDtypeStruct((b, t, d), F32),
            jax.ShapeDtypeStruct((b, CONV_TAPS - 1, r), F32),
            jax.ShapeDtypeStruct((b, 1, r), F32),
        ],
        scratch_shapes=[
            pltpu.VMEM((tt + SUBLANES, r), F32),
            pltpu.VMEM((tt, r), F32), pltpu.VMEM((tt, r), F32), pltpu.VMEM((tt, r), F32),
            pltpu.VMEM((tt, r), F32), pltpu.VMEM((tt, r), F32),
            pltpu.VMEM((1, r), F32),
        ],
        compiler_params=_params(2),
        name="lru_layer",
    )(x, mod, gpre, gpost, w_in, conv_w, conv_b, w_ax, b_a, b_x, lam, w_out, conv0, h0)


def _split3(x):
    x1 = x.astype(BF16)
    r1 = x - x1.astype(F32)
    x2 = r1.astype(BF16)
    x3 = (r1 - x2.astype(F32)).astype(BF16)
    return x1, x2, x3


def _tri_cumsum(tri, x, carry):
    c = carry
    for piece in _split3(x):
        c = c + jnp.dot(tri, piece, preferred_element_type=F32)
    return c


def _fox_in_kernel(x_ref, mod_ref, gpre_ref, w_ref, wvt_ref, wf_ref, bf_ref, tri_ref, hsel_ref,
                   q_ref, kb_ref, k_ref, v_ref, sg_ref, vt_ref, lf_ref, cum_ref, aug_ref,
                   qn_ref, kn_ref, car, *, w, nh, qscale):
    @pl.when(pl.program_id(1) == 0)
    def _():
        car[...] = jnp.zeros_like(car)

    hb = _modulated_norm(x_ref[0], mod_ref[0], gpre_ref[...]).astype(BF16)
    z = jnp.dot(hb, w_ref[...], preferred_element_type=F32)
    q = z[:, :w] * qscale
    q_ref[0] = q.astype(BF16)
    k = z[:, w:2 * w]
    k_ref[0] = k
    kb_ref[0] = k.astype(BF16)
    hsel = hsel_ref[...]
    qn_ref[0, 0] = jnp.max(jnp.dot((q * q).astype(BF16), hsel, preferred_element_type=F32),
                           axis=0, keepdims=True)
    kn_ref[0, 0] = jnp.max(jnp.dot((k * k).astype(BF16), hsel, preferred_element_type=F32),
                           axis=0, keepdims=True)
    v_ref[0] = z[:, 2 * w:3 * w]
    sg_ref[0] = _silu(z[:, 3 * w:]).astype(BF16)
    vt = lax.dot_general(wvt_ref[...], hb, _NT, preferred_element_type=F32)
    tt = vt.shape[1]
    vt = vt.reshape(nh, HEAD_DIM, tt)
    vt_ref[0, :, 0] = jnp.concatenate([vt, jnp.ones((nh, ONES_ROWS, tt), F32)], axis=1).astype(BF16)
    fl = jnp.dot(hb, wf_ref[...], preferred_element_type=F32) + bf_ref[...]
    lf = -_softplus(-fl)
    c = _tri_cumsum(tri_ref[...], lf, car[...])
    tt = c.shape[0]
    car[...] = c[tt - 1:tt, :]
    lf_ref[0] = lf[:, :nh]
    cum_ref[0] = c[:, :nh]
    y1, y2, y3 = _split3(c * (-LOG2E))
    lane = lax.broadcasted_iota(jnp.int32, c.shape, 1)
    aug = jnp.where(lane < nh, y1.astype(F32),
                    jnp.where(lane < 2 * nh, y2.astype(F32),
                              jnp.where(lane < 3 * nh, y3.astype(F32), 0.0)))
    aug_ref[0] = aug.astype(BF16)


def _tri(n):
    return jnp.tril(jnp.ones((n, n), F32)).astype(BF16)


def _fox_in(x, mod, mod_off, gpre, w_main, w_vt, w_f3, b_f3, nh, *, tt, qscale):
    b, t, d = x.shape
    w = w_main.shape[1] // 4
    rows = HEAD_DIM + ONES_ROWS
    kern = functools.partial(_fox_in_kernel, w=w, nh=nh, qscale=qscale)
    tile = lambda dt: jax.ShapeDtypeStruct((b, t, w), dt)
    tile_spec = pl.BlockSpec((1, tt, w), lambda i, j: (i, j, 0))
    head_spec = pl.BlockSpec((1, tt, nh), lambda i, j: (i, j, 0))
    norm_spec = pl.BlockSpec((1, 1, 1, LANES), lambda i, j: (i, j, 0, 0))
    hsel = (jnp.arange(w)[:, None] // HEAD_DIM == jnp.arange(LANES)[None, :]).astype(BF16)
    return pl.pallas_call(
        kern,
        grid=(b, t // tt),
        in_specs=[
            pl.BlockSpec((1, tt, d), lambda i, j: (i, j, 0)),
            pl.BlockSpec((1, 3, d), lambda i, j: (i + mod_off, 0, 0)),
            _const_spec((1, d)),
            _const_spec((d, 4 * w)),
            _const_spec((w, d)),
            _const_spec((d, LANES)),
            _const_spec((1, LANES)),
            _const_spec((tt, tt)),
            _const_spec((w, LANES)),
        ],
        out_specs=[tile_spec] * 5 + [pl.BlockSpec((1, nh, 1, rows, tt), lambda i, j: (i, 0, j, 0, 0)),
                                     head_spec, head_spec,
                                     pl.BlockSpec((1, tt, LANES), lambda i, j: (i, j, 0)),
                                     norm_spec, norm_spec],
        out_shape=[tile(BF16), tile(BF16), tile(F32), tile(F32), tile(BF16),
                   jax.ShapeDtypeStruct((b, nh, t // tt, rows, tt), BF16),
                   jax.ShapeDtypeStruct((b, t, nh), F32), jax.ShapeDtypeStruct((b, t, nh), F32),
                   jax.ShapeDtypeStruct((b, t, LANES), BF16),
                   jax.ShapeDtypeStruct((b, t // tt, 1, LANES), F32),
                   jax.ShapeDtypeStruct((b, t // tt, 1, LANES), F32)],
        scratch_shapes=[pltpu.VMEM((1, LANES), F32)],
        compiler_params=_params(2),
        name="fox_in",
    )(x, mod, gpre, w_main, w_vt, w_f3, b_f3, _tri(tt), hsel)


def _cumsum_kernel(x_ref, tri_ref, o_ref, car):
    @pl.when(pl.program_id(1) == 0)
    def _():
        car[...] = jnp.zeros_like(car)

    c = _tri_cumsum(tri_ref[...], x_ref[0], car[...])
    o_ref[0] = c
    tc = c.shape[0]
    car[...] = c[tc - 1:tc, :]


def _cumsum_time(x, *, tc):
    b, t, nh = x.shape
    return pl.pallas_call(
        _cumsum_kernel,
        grid=(b, t // tc),
        in_specs=[pl.BlockSpec((1, tc, nh), lambda i, j: (i, j, 0)), _const_spec((tc, tc))],
        out_specs=pl.BlockSpec((1, tc, nh), lambda i, j: (i, j, 0)),
        out_shape=jax.ShapeDtypeStruct((b, t, nh), F32),
        scratch_shapes=[pltpu.VMEM((1, nh), F32)],
        compiler_params=_params(2),
        name="cumsum_time",
    )(x, _tri(tc))


def _split_heads(q):
    qf = q.astype(F32)
    lane = lax.broadcasted_iota(jnp.int32, qf.shape, 1)
    return (jnp.where(lane < HEAD_DIM, qf, 0.0).astype(BF16),
            jnp.where(lane >= HEAD_DIM, qf, 0.0).astype(BF16))


def _attn_kernel(cnt_ref, q_ref, k_ref, a_ref, vt_ref, o_ref, z0, z1, zm0, zm1, p0, p1, al0, al1,
                 qa_s, m_s, acc_s, *, tq, tk, nh):
    bi = pl.program_id(0)
    hp = pl.program_id(1)
    qi = pl.program_id(2)
    nq = pl.num_programs(2)
    z_s, zmax_s, p_s, alpha_s = (z0, z1), (zm0, zm1), (p0, p1), (al0, al1)

    @pl.when((bi == 0) & (hp == 0) & (qi == 0))
    def _():
        for p_ref, al_ref in zip(p_s, alpha_s):
            p_ref[...] = jnp.zeros_like(p_ref)
            al_ref[...] = jnp.ones_like(al_ref)

    q = q_ref[0].astype(F32)
    lane = lax.broadcasted_iota(jnp.int32, (tq, LANES), 1)
    for hh in range(2):
        head = 2 * hp + hh
        own = (lane >= hh * HEAD_DIM) & (lane < (hh + 1) * HEAD_DIM)
        pick = (lane == head) | (lane == head + nh) | (lane == head + 2 * nh)
        qa_s[hh] = jnp.concatenate(
            [jnp.where(own, q, 0.0).astype(BF16), jnp.where(pick, 1.0, 0.0).astype(BF16)], axis=1)
    m_s[...] = jnp.full_like(m_s, NEG_BIG)
    acc_s[...] = jnp.zeros_like(acc_s)
    tv = vt_ref.shape[4]
    nsub = tk // tv

    n0 = cnt_ref[(bi * nh + 2 * hp) * nq + qi]
    n_off = n0 + cnt_ref[(bi * nh + 2 * hp + 1) * nq + qi]
    last = n_off + 1

    def item(t):
        u = jnp.maximum(t, 0) - 2
        hh = jnp.where(u < 0, u + 2, (u >= n0).astype(jnp.int32))
        off = jnp.where(u < 0, 0, u - hh * n0 + 1)
        return hh, qi - off

    def qk(t, slot, diagonal_head=None):
        if diagonal_head is None:
            hh, blk = item(t)
        else:
            hh, blk = diagonal_head, qi
        st = pl.multiple_of(blk * tk, tk)
        keys = jnp.concatenate([k_ref[0, pl.ds(st, tk), :], a_ref[0, pl.ds(st, tk), :]], axis=1)
        z = lax.dot_general(keys, qa_s[hh], _NT, preferred_element_type=F32)
        if diagonal_head is not None:
            kpos = lax.broadcasted_iota(jnp.int32, z.shape, 0)
            qpos = lax.broadcasted_iota(jnp.int32, z.shape, 1)
            z = jnp.where(kpos <= qpos, z, NEG_BIG)
        z_s[slot][...] = z
        zmax_s[slot][...] = jnp.max(z, axis=0, keepdims=True)

    def softmax(t, slot):
        hh, _ = item(t)
        m_prev = m_s[hh]
        m_next = jnp.maximum(m_prev, zmax_s[slot][...])
        p_s[slot][...] = jnp.exp2(z_s[slot][...] - m_next).astype(BF16)
        alpha_s[slot][...] = jnp.exp2(m_prev - m_next)
        m_s[hh] = m_next

    def pv(t, slot, valid):
        hh, blk = item(t)
        r = None
        for u in range(nsub):
            part = jnp.dot(vt_ref[0, hh, blk * nsub + u], p_s[slot][u * tv:(u + 1) * tv, :],
                           preferred_element_type=F32)
            r = part if r is None else r + part
        if valid is not None:
            r = jnp.where(valid, r, 0.0)
        acc_s[hh] = acc_s[hh] * alpha_s[slot][...] + r

    def iteration(t, slot):
        pv(t - 2, slot, t >= 2)
        softmax(t, slot)
        qk(t + 2, slot)

    def finish(slot):
        pv(last - 3, slot, last >= 3)
        softmax(last - 1, slot)
        pv(last - 2, 1 - slot, last >= 2)
        softmax(last, 1 - slot)
        pv(last - 1, slot, None)
        pv(last, 1 - slot, None)
        outs = []
        for hh in range(2):
            acc = acc_s[hh]
            outs.append(acc[:HEAD_DIM] / acc[HEAD_DIM:HEAD_DIM + 1])
        o_ref[0] = jnp.concatenate(outs, axis=0).T.astype(o_ref.dtype)

    qk(0, 0, 0)
    qk(1, 1, 1)

    def pair(k, carry):
        iteration(2 * k, 0)
        iteration(2 * k + 1, 1)
        return carry

    lax.fori_loop(0, n_off // 2, pair, 0)
    odd = lax.rem(n_off, 2) == 1

    @pl.when(odd)
    def _():
        iteration(n_off - 1, 0)
        finish(1)

    @pl.when(jnp.logical_not(odd))
    def _():
        finish(0)


def _attention(cnt, qb, kb, aug, vt, nh, *, tq, tk):
    assert tq == tk, "one key block per query tile on the diagonal"
    b, t, w = qb.shape
    hp = w // LANES
    nq = t // tq
    _, _, nv, rows, tv = vt.shape
    kern = functools.partial(_attn_kernel, tq=tq, tk=tk, nh=nh)
    grid_spec = pltpu.PrefetchScalarGridSpec(
        num_scalar_prefetch=1,
        grid=(b, hp, nq),
        in_specs=[
            pl.BlockSpec((1, tq, LANES), lambda i, h, j, c: (i, j, h)),
            pl.BlockSpec((1, t, LANES), lambda i, h, j, c: (i, 0, h)),
            pl.BlockSpec((1, t, LANES), lambda i, h, j, c: (i, 0, 0)),
            pl.BlockSpec((1, 2, nv, rows, tv), lambda i, h, j, c: (i, h, 0, 0, 0)),
        ],
        out_specs=pl.BlockSpec((1, tq, LANES), lambda i, h, j, c: (i, j, h)),
        scratch_shapes=(
            [pltpu.VMEM((tk, tq), F32)] * 2
            + [pltpu.VMEM((1, tq), F32)] * 2
            + [pltpu.VMEM((tk, tq), BF16)] * 2
            + [pltpu.VMEM((1, tq), F32)] * 2
            + [pltpu.VMEM((2, tq, 2 * LANES), BF16),
               pltpu.VMEM((2, 1, tq), F32),
               pltpu.VMEM((2, rows, tq), F32)]))
    return pl.pallas_call(
        kern,
        grid_spec=grid_spec,
        out_shape=jax.ShapeDtypeStruct((b, t, w), BF16),
        compiler_params=_params(3),
        name="fox_attention",
    )(cnt, qb, kb, aug, vt)


EXP2_ZERO_BELOW = 150.0


def _plan_kernel(qn_ref, kn_ref, cend_ref, cnt_ref, *, nh, ndiag):
    qn2 = jnp.max(qn_ref[0], axis=1)[:, :nh]
    kn2 = jnp.max(kn_ref[0], axis=0, keepdims=True)[:, :nh]
    bend = cend_ref[0] * (-LOG2E)
    nk = bend.shape[0]
    reach = 2.04 * jnp.sqrt(qn2 * kn2)
    blk = lax.broadcasted_iota(jnp.int32, (nk, nh), 0)
    rows = []
    for qi in range(qn2.shape[0]):
        d0 = qi * ndiag
        bprev = bend[d0 - 1:d0] if d0 > 0 else jnp.zeros((1, nh), F32)
        slack = 2.0 + 1e-5 * (jnp.abs(bprev) + jnp.abs(bend))
        live = (blk < d0) & (bprev - bend <= EXP2_ZERO_BELOW + reach[qi:qi + 1] + slack)
        rows.append(jnp.max(jnp.where(live, d0 - blk, 0), axis=0, keepdims=True))
    cnt_ref[0] = jnp.concatenate(rows, axis=0)


def _plan(qn2, kn2, cend, nh):
    b, nq, sub, _ = qn2.shape
    nt = kn2.shape[1]
    nk = cend.shape[1]
    return pl.pallas_call(
        functools.partial(_plan_kernel, nh=nh, ndiag=nk // nq),
        grid=(b,),
        in_specs=[
            pl.BlockSpec((1, nq, sub, LANES), lambda i: (i, 0, 0, 0)),
            pl.BlockSpec((1, nt, LANES), lambda i: (i, 0, 0)),
            pl.BlockSpec((1, nk, nh), lambda i: (i, 0, 0)),
        ],
        out_specs=pl.BlockSpec((1, nq, nh), lambda i: (i, 0, 0)),
        out_shape=jax.ShapeDtypeStruct((b, nq, nh), jnp.int32),
        compiler_params=_params(1),
        name="fox_plan",
    )(qn2, kn2, cend)


def _attn_cached_kernel(q_ref, kn_ref, vn_ref, pk_ref, pv_ref, cp_ref, cn_ref, o_ref):
    qh = _split_heads(q_ref[0])
    pkb = pk_ref[0].astype(BF16)
    pvb = pv_ref[0].astype(BF16)
    kn = kn_ref[0]
    vn = vn_ref[0]
    t = kn.shape[0]
    p_len = pkb.shape[0]
    rowi = lax.broadcasted_iota(jnp.int32, (t, t), 0)
    coli = lax.broadcasted_iota(jnp.int32, (t, t), 1)
    outs = []
    for hh in range(2):
        cp = cp_ref[0, 0, hh:hh + 1, :]
        tot = cp[:, p_len - 1:p_len]
        z1 = lax.dot_general(qh[hh], pkb, _NT, preferred_element_type=F32) - (cp - tot) * LOG2E
        z2 = (lax.dot_general(qh[hh], kn, _NT, preferred_element_type=F32)
              - cn_ref[0, 0, hh:hh + 1, :] * LOG2E)
        z2 = jnp.where(coli <= rowi, z2, NEG_BIG)
        m = jnp.maximum(jnp.max(z1, axis=1), jnp.max(z2, axis=1))[:, None]
        p1 = jnp.exp2(z1 - m)
        p2 = jnp.exp2(z2 - m)
        l = (jnp.sum(p1, axis=1) + jnp.sum(p2, axis=1))[:, None]
        o = (jnp.dot(p1.astype(BF16), pvb, preferred_element_type=F32)
             + jnp.dot(p2.astype(BF16), vn, preferred_element_type=F32))
        outs.append(o / l)
    lane = lax.broadcasted_iota(jnp.int32, (t, LANES), 1)
    o_ref[0] = jnp.where(lane < HEAD_DIM, outs[0], outs[1]).astype(o_ref.dtype)


def _attention_cached(qb, kb, vb, past_k, past_v, cum_past, cum_new):
    b, t, w = qb.shape
    p_len = past_k.shape[1]
    hp = w // LANES
    new_spec = pl.BlockSpec((1, t, LANES), lambda i, h: (i, 0, h))
    past_spec = pl.BlockSpec((1, p_len, LANES), lambda i, h: (i, 0, h))
    return pl.pallas_call(
        _attn_cached_kernel,
        grid=(b, hp),
        in_specs=[new_spec, new_spec, new_spec, past_spec, past_spec,
                  pl.BlockSpec((1, 1, 2, p_len), lambda i, h: (i, h, 0, 0)),
                  pl.BlockSpec((1, 1, 2, t), lambda i, h: (i, h, 0, 0))],
        out_specs=new_spec,
        out_shape=jax.ShapeDtypeStruct((b, t, w), BF16),
        compiler_params=_params(2),
        name="fox_attention_cached",
    )(qb, kb, vb, past_k, past_v,
      cum_past.reshape(b, hp, 2, p_len), cum_new.reshape(b, hp, 2, t))


def _fox_out_kernel(o_ref, sg_ref, x_ref, mod_ref, gpost_ref, w_ref, y_ref):
    yv = (o_ref[0].astype(F32) * sg_ref[0].astype(F32)).astype(BF16)
    y = jnp.dot(yv, w_ref[...], preferred_element_type=F32)
    y_ref[0] = _gated_residual(x_ref[0], y, mod_ref[0], gpost_ref[...])


def _fox_out(o, g, x, mod, mod_off, gpost, w_out, *, tt):
    b, t, d = x.shape
    w = o.shape[-1]
    return pl.pallas_call(
        _fox_out_kernel,
        grid=(b, t // tt),
        in_specs=[
            pl.BlockSpec((1, tt, w), lambda i, j: (i, j, 0)),
            pl.BlockSpec((1, tt, w), lambda i, j: (i, j, 0)),
            pl.BlockSpec((1, tt, d), lambda i, j: (i, j, 0)),
            pl.BlockSpec((1, 3, d), lambda i, j: (i + mod_off, 0, 0)),
            _const_spec((1, d)),
            _const_spec((w, d)),
        ],
        out_specs=pl.BlockSpec((1, tt, d), lambda i, j: (i, j, 0)),
        out_shape=jax.ShapeDtypeStruct((b, t, d), F32),
        compiler_params=_params(2),
        name="fox_out",
    )(o, g, x, mod, gpost, w_out)


def _tile(t, pref):
    return pref if t % pref == 0 else t


def kernel(x_prompt, x_sample, c_prompt, c_sample, state_lru_h, state_lru_conv, cache_fox_k, cache_fox_v, cache_fox_logf, norm_pre, norm_post, ada_w, ada_b, lru_w_in, lru_conv_w, lru_conv_b, lru_w_a, lru_b_a, lru_w_x, lru_b_x, lru_lambda, lru_w_out, fox_w_in, fox_b_f, fox_w_out):
    b, t, d = x_prompt.shape
    db, dt, _ = x_sample.shape
    depth = ada_w.shape[0]
    r = lru_lambda.shape[-1]
    nh = fox_b_f.shape[-1]
    w = fox_w_out.shape[1]
    p_len = cache_fox_k.shape[2]
    scale = float(HEAD_DIM) ** -0.5

    c_all = jnp.concatenate([c_prompt, c_sample], axis=0)
    mod = _adaln(c_all, ada_w, ada_b).reshape(depth, b + db, 3, d)

    xp, xs = x_prompt, x_sample

    j = 0
    w_ax = jnp.concatenate([lru_w_a[j], lru_w_x[j]], axis=-1).astype(BF16)
    lru_args = (norm_pre[0:1], norm_post[0:1], lru_w_in[j].astype(BF16), lru_conv_w[j],
                lru_conv_b[j:j + 1], w_ax, lru_b_a[j:j + 1], lru_b_x[j:j + 1],
                lru_lambda[j:j + 1], lru_w_out[j].astype(BF16))
    xp, conv_p, h_p = _lru_layer(
        xp, mod[0], 0, *lru_args,
        jnp.zeros((b, CONV_TAPS - 1, r), F32), jnp.zeros((b, 1, r), F32), tt=_tile(t, 512))
    xs, conv_s, h_s = _lru_layer(
        xs, mod[0], b, *lru_args,
        state_lru_conv[j], state_lru_h[j].reshape(db, 1, r), tt=_tile(dt, 256))

    w_main = fox_w_in[j][:, :4 * w].astype(BF16)
    pad = jnp.zeros((d, LANES - 3 * nh), F32)
    w_f3 = jnp.concatenate([fox_w_in[j][:, 4 * w:]] * 3 + [pad], axis=1).astype(BF16)
    b_f3 = jnp.concatenate([fox_b_f[j]] * 3 + [jnp.zeros((LANES - 3 * nh,), F32)]).reshape(1, LANES)
    w_out = fox_w_out[j].astype(BF16)
    qscale = scale * LOG2E

    w_vt = fox_w_in[j][:, 2 * w:3 * w].T.astype(BF16)
    tt = _tile(t, 256)
    tk = _tile(t, 512)
    tq = tk
    qb, kb, k_p, v_p, sg_p, vt_p, lf_p, cum_p, aug_p, qn_p, kn_p = _fox_in(
        xp, mod[1], 0, norm_pre[1:2], w_main, w_vt, w_f3, b_f3, nh, tt=tt, qscale=qscale)
    cnt = _plan(qn_p.reshape(b, t // tq, tq // tt, LANES), kn_p.reshape(b, t // tt, LANES),
                cum_p[:, tk - 1::tk, :], nh)
    o_p = _attention(cnt.transpose(0, 2, 1).reshape(-1), qb, kb, aug_p, vt_p, nh, tq=tq, tk=tk)
    yp = _fox_out(o_p, sg_p, xp, mod[1], 0, norm_post[1:2], w_out, tt=_tile(t, 512))

    qs, ksb, k_s, v_s, sg_s, _, lf_s, cum_s, _, _, _ = _fox_in(
        xs, mod[1], b, norm_pre[1:2], w_main, w_vt, w_f3, b_f3, nh, tt=_tile(dt, 256), qscale=qscale)
    lf_past = cache_fox_logf[j].transpose(1, 0, 2).reshape(1, p_len, db * nh)
    cum_past = _cumsum_time(lf_past, tc=_tile(p_len, 512)).reshape(p_len, db, nh).transpose(1, 2, 0)
    o_s = _attention_cached(qs, ksb, v_s.astype(BF16),
                            cache_fox_k[j].reshape(db, p_len, w),
                            cache_fox_v[j].reshape(db, p_len, w),
                            cum_past, jnp.swapaxes(cum_s, 1, 2))
    ys = _fox_out(o_s, sg_s, xs, mod[1], b, norm_post[1:2], w_out, tt=_tile(dt, 512))

    hd = (nh, HEAD_DIM)
    return (yp, ys,
            h_p.reshape(1, b, r), conv_p[None],
            k_p.reshape(1, b, t, *hd), v_p.reshape(1, b, t, *hd), lf_p[None],
            h_s.reshape(1, db, r), conv_s[None],
            k_s.reshape(1, db, dt, *hd), v_s.reshape(1, db, dt, *hd), lf_s[None])
```

```python
import functools

import jax
import jax.numpy as jnp
from jax import lax
from jax.experimental import pallas as pl
from jax.experimental.pallas import tpu as pltpu

F32 = jnp.float32
BF16 = jnp.bfloat16

EPS = 1e-6
LRU_C = 8.0
CONV_TAPS = 4
HEAD_DIM = 64
LANES = 128
SUBLANES = 8
NEG_BIG = -1e30
LOG2E = 1.4426950408889634
ONES_ROWS = 16
VMEM_LIMIT_BYTES = 56 * 1024 * 1024

_NT = (((1,), (1,)), ((), ()))


def _params(n_grid):
    return pltpu.CompilerParams(
        dimension_semantics=("arbitrary",) * n_grid,
        vmem_limit_bytes=VMEM_LIMIT_BYTES)


def _const_spec(shape):
    nd = len(shape)
    return pl.BlockSpec(shape, lambda *_: (0,) * nd, pipeline_mode=pl.Buffered(1))


def _sigmoid(x):
    return 0.5 * jnp.tanh(0.5 * x) + 0.5


def _silu(x):
    return x * _sigmoid(x)


def _softplus(x):
    return jnp.maximum(x, 0.0) + jnp.log1p(jnp.exp(-jnp.abs(x)))


def _one_minus_exp_neg(t, exp_neg_t):
    p = jnp.full_like(t, -1.0 / 24.0)
    for c in (1.0 / 6.0, -0.5, 1.0):
        p = p * t + c
    return jnp.where(t < 1.0 / 64.0, t * p, 1.0 - exp_neg_t)


def _sqrt_pos(x):
    x = jnp.maximum(x, 1e-30)
    return x * lax.rsqrt(x)


def _modulated_norm(x, mod, g):
    ms = jnp.mean(x * x, axis=-1, keepdims=True)
    h = x * lax.rsqrt(ms + EPS) * g
    return h * (1.0 + mod[1:2]) + mod[0:1]


def _gated_residual(x, y, mod, g):
    ms = jnp.mean(y * y, axis=-1, keepdims=True)
    return x + mod[2:3] * (y * lax.rsqrt(ms + EPS) * g)


def _adaln_kernel(c_ref, w_ref, b_ref, o_ref):
    sc = _silu(c_ref[...]).astype(BF16)
    o_ref[0] = jnp.dot(sc, w_ref[0].astype(BF16), preferred_element_type=F32) + b_ref[0]


def _adaln(c_all, ada_w, ada_b):
    depth, d, d3 = ada_w.shape
    n = c_all.shape[0]
    nj = d3 // d
    return pl.pallas_call(
        _adaln_kernel,
        grid=(depth, nj),
        in_specs=[
            pl.BlockSpec((n, d), lambda i, j: (0, 0)),
            pl.BlockSpec((1, d, d), lambda i, j: (i, 0, j)),
            pl.BlockSpec((1, 1, d), lambda i, j: (i, 0, j)),
        ],
        out_specs=pl.BlockSpec((1, n, d), lambda i, j: (i, 0, j)),
        out_shape=jax.ShapeDtypeStruct((depth, n, d3), F32),
        compiler_params=_params(2),
        name="adaln",
    )(c_all, ada_w, ada_b.reshape(depth, 1, d3))


def _lru_kernel(x_ref, mod_ref, gpre_ref, gpost_ref, win_ref, cw_ref, cb_ref, wax_ref,
                ba_ref, bx_ref, lam_ref, wout_ref, conv0_ref, h0_ref,
                y_ref, convo_ref, hlast_ref,
                xbuf, xc_s, pre_a, pre_x, gate_s, yv_s, hcar, *, tt, r, nblk):
    t = pl.program_id(1)
    nt = pl.num_programs(1)
    tail0 = SUBLANES - (CONV_TAPS - 1)

    @pl.when(t == 0)
    def _():
        xbuf[tail0:SUBLANES, :] = conv0_ref[0]
        hcar[...] = h0_ref[0]

    mod = mod_ref[0]
    cw = cw_ref[...]
    half = tt // 2

    def front(lo):
        h = _modulated_norm(x_ref[0, lo:lo + half, :], mod, gpre_ref[...])
        z = jnp.dot(h.astype(BF16), win_ref[...], preferred_element_type=F32)
        xbuf[SUBLANES + lo:SUBLANES + lo + half, :] = z[:, :r]
        gate_s[lo:lo + half, :] = z[:, r:]
        xc = cb_ref[...] + xbuf[tail0 + lo:tail0 + lo + half, :] * cw[0:1]
        for k in range(1, CONV_TAPS):
            xc = xc + xbuf[tail0 + k + lo:tail0 + k + lo + half, :] * cw[k:k + 1]
        xc_s[lo:lo + half, :] = xc
        xcb = xc.astype(BF16)
        for n in range(nblk):
            c0, c1 = n * LANES, (n + 1) * LANES
            res = jnp.dot(xcb[:, c0:c1], wax_ref[n], preferred_element_type=F32)
            pre_a[lo:lo + half, c0:c1] = res[:, :LANES]
            pre_x[lo:lo + half, c0:c1] = res[:, LANES:]

    coef = -LRU_C * _softplus(-lam_ref[...])
    coef2n = -2.0 * coef
    ba = ba_ref[...]
    bx = bx_ref[...]
    row = lax.broadcasted_iota(jnp.int32, (SUBLANES, r), 0)

    def group(st, hprev):
        sl = slice(st, st + SUBLANES)
        rg = _sigmoid(pre_a[sl, :] + ba)
        ig = _sigmoid(pre_x[sl, :] + bx)
        a = jnp.exp(coef * rg)
        u = _sqrt_pos(_one_minus_exp_neg(coef2n * rg, a * a)) * (ig * xc_s[sl, :])
        for s in (1, 2, 4):
            a_sh = jnp.where(row >= s, pltpu.roll(a, s, axis=0), 1.0)
            u_sh = jnp.where(row >= s, pltpu.roll(u, s, axis=0), 0.0)
            u = a * u_sh + u
            a = a * a_sh
        hs = a * hprev + u
        yv_s[sl, :] = hs * _silu(gate_s[sl, :])
        return hs[SUBLANES - 1:SUBLANES, :]

    def scan(lo, hprev):
        for st in range(lo, lo + half, SUBLANES):
            hprev = group(st, hprev)
        return hprev

    def back(lo):
        y = jnp.dot(yv_s[lo:lo + half, :].astype(BF16), wout_ref[...], preferred_element_type=F32)
        y_ref[0, lo:lo + half, :] = _gated_residual(x_ref[0, lo:lo + half, :], y, mod, gpost_ref[...])

    front(0)
    front(half)
    tail = xbuf[tt + tail0:tt + SUBLANES, :]
    hmid = scan(0, hcar[...])
    back(0)
    hlast = scan(half, hmid)
    back(half)
    xbuf[tail0:SUBLANES, :] = tail
    hcar[...] = hlast

    @pl.when(t == nt - 1)
    def _():
        convo_ref[0] = tail
        hlast_ref[0] = hlast


def _lru_layer(x, mod, mod_off, gpre, gpost, w_in, conv_w, conv_b, w_ax, b_a, b_x, lam, w_out,
               conv0, h0, *, tt):
    b, t, d = x.shape
    r = lam.shape[-1]
    nblk = w_ax.shape[0]
    kern = functools.partial(_lru_kernel, tt=tt, r=r, nblk=nblk)
    return pl.pallas_call(
        kern,
        grid=(b, t // tt),
        in_specs=[
            pl.BlockSpec((1, tt, d), lambda i, j: (i, j, 0)),
            pl.BlockSpec((1, 3, d), lambda i, j: (i + mod_off, 0, 0)),
            _const_spec((1, d)), _const_spec((1, d)),
            _const_spec((d, 2 * r)),
            _const_spec((CONV_TAPS, r)), _const_spec((1, r)),
            _const_spec((nblk, LANES, 2 * LANES)),
            _const_spec((1, r)), _const_spec((1, r)), _const_spec((1, r)),
            _const_spec((r, d)),
            pl.BlockSpec((1, CONV_TAPS - 1, r), lambda i, j: (i, 0, 0)),
            pl.BlockSpec((1, 1, r), lambda i, j: (i, 0, 0)),
        ],
        out_specs=[
            pl.BlockSpec((1, tt, d), lambda i, j: (i, j, 0)),
            pl.BlockSpec((1, CONV_TAPS - 1, r), lambda i, j: (i, 0, 0)),
            pl.BlockSpec((1, 1, r), lambda i, j: (i, 0, 0)),
        ],
        out_shape=[
            jax.ShapeDtypeStruct((b, t, d), F32),
            jax.ShapeDtypeStruct((b, CONV_TAPS - 1, r), F32),
            jax.ShapeDtypeStruct((b, 1, r), F32),
        ],
        scratch_shapes=[
            pltpu.VMEM((tt + SUBLANES, r), F32),
            pltpu.VMEM((tt, r), F32), pltpu.VMEM((tt, r), F32), pltpu.VMEM((tt, r), F32),
            pltpu.VMEM((tt, r), F32), pltpu.VMEM((tt, r), F32),
            pltpu.VMEM((1, r), F32),
        ],
        compiler_params=_params(2),
        name="lru_layer",
    )(x, mod, gpre, gpost, w_in, conv_w, conv_b, w_ax, b_a, b_x, lam, w_out, conv0, h0)


def _split3(x):
    x1 = x.astype(BF16)
    r1 = x - x1.astype(F32)
    x2 = r1.astype(BF16)
    x3 = (r1 - x2.astype(F32)).astype(BF16)
    return x1, x2, x3


def _tri_cumsum(tri, x, carry):
    c = carry
    for piece in _split3(x):
        c = c + jnp.dot(tri, piece, preferred_element_type=F32)
    return c


def _fox_in_kernel(x_ref, mod_ref, gpre_ref, w_ref, wvt_ref, wf_ref, bf_ref, tri_ref, hsel_ref,
                   q_ref, kb_ref, k_ref, v_ref, sg_ref, vt_ref, lf_ref, cum_ref, aug_ref,
                   qn_ref, kn_ref, car, *, w, nh, qscale):
    @pl.when(pl.program_id(1) == 0)
    def _():
        car[...] = jnp.zeros_like(car)

    hb = _modulated_norm(x_ref[0], mod_ref[0], gpre_ref[...]).astype(BF16)
    z = jnp.dot(hb, w_ref[...], preferred_element_type=F32)
    q = z[:, :w] * qscale
    q_ref[0] = q.astype(BF16)
    k = z[:, w:2 * w]
    k_ref[0] = k
    kb_ref[0] = k.astype(BF16)
    hsel = hsel_ref[...]
    qn_ref[0, 0] = jnp.max(jnp.dot((q * q).astype(BF16), hsel, preferred_element_type=F32),
                           axis=0, keepdims=True)
    kn_ref[0, 0] = jnp.max(jnp.dot((k * k).astype(BF16), hsel, preferred_element_type=F32),
                           axis=0, keepdims=True)
    v_ref[0] = z[:, 2 * w:3 * w]
    sg_ref[0] = _silu(z[:, 3 * w:]).astype(BF16)
    vt = lax.dot_general(wvt_ref[...], hb, _NT, preferred_element_type=F32)
    tt = vt.shape[1]
    vt = vt.reshape(nh, HEAD_DIM, tt)
    vt_ref[0, :, 0] = jnp.concatenate([vt, jnp.ones((nh, ONES_ROWS, tt), F32)], axis=1).astype(BF16)
    fl = jnp.dot(hb, wf_ref[...], preferred_element_type=F32) + bf_ref[...]
    lf = -_softplus(-fl)
    c = _tri_cumsum(tri_ref[...], lf, car[...])
    tt = c.shape[0]
    car[...] = c[tt - 1:tt, :]
    lf_ref[0] = lf[:, :nh]
    cum_ref[0] = c[:, :nh]
    y1, y2, y3 = _split3(c * (-LOG2E))
    lane = lax.broadcasted_iota(jnp.int32, c.shape, 1)
    aug = jnp.where(lane < nh, y1.astype(F32),
                    jnp.where(lane < 2 * nh, y2.astype(F32),
                              jnp.where(lane < 3 * nh, y3.astype(F32), 0.0)))
    aug_ref[0] = aug.astype(BF16)


def _tri(n):
    return jnp.tril(jnp.ones((n, n), F32)).astype(BF16)


def _fox_in(x, mod, mod_off, gpre, w_main, w_vt, w_f3, b_f3, nh, *, tt, qscale):
    b, t, d = x.shape
    w = w_main.shape[1] // 4
    rows = HEAD_DIM + ONES_ROWS
    kern = functools.partial(_fox_in_kernel, w=w, nh=nh, qscale=qscale)
    tile = lambda dt: jax.ShapeDtypeStruct((b, t, w), dt)
    tile_spec = pl.BlockSpec((1, tt, w), lambda i, j: (i, j, 0))
    head_spec = pl.BlockSpec((1, tt, nh), lambda i, j: (i, j, 0))
    norm_spec = pl.BlockSpec((1, 1, 1, LANES), lambda i, j: (i, j, 0, 0))
    hsel = (jnp.arange(w)[:, None] // HEAD_DIM == jnp.arange(LANES)[None, :]).astype(BF16)
    return pl.pallas_call(
        kern,
        grid=(b, t // tt),
        in_specs=[
            pl.BlockSpec((1, tt, d), lambda i, j: (i, j, 0)),
            pl.BlockSpec((1, 3, d), lambda i, j: (i + mod_off, 0, 0)),
            _const_spec((1, d)),
            _const_spec((d, 4 * w)),
            _const_spec((w, d)),
            _const_spec((d, LANES)),
            _const_spec((1, LANES)),
            _const_spec((tt, tt)),
            _const_spec((w, LANES)),
        ],
        out_specs=[tile_spec] * 5 + [pl.BlockSpec((1, nh, 1, rows, tt), lambda i, j: (i, 0, j, 0, 0)),
                                     head_spec, head_spec,
                                     pl.BlockSpec((1, tt, LANES), lambda i, j: (i, j, 0)),
                                     norm_spec, norm_spec],
        out_shape=[tile(BF16), tile(BF16), tile(F32), tile(F32), tile(BF16),
                   jax.ShapeDtypeStruct((b, nh, t // tt, rows, tt), BF16),
                   jax.ShapeDtypeStruct((b, t, nh), F32), jax.ShapeDtypeStruct((b, t, nh), F32),
                   jax.ShapeDtypeStruct((b, t, LANES), BF16),
                   jax.ShapeDtypeStruct((b, t // tt, 1, LANES), F32),
                   jax.ShapeDtypeStruct((b, t // tt, 1, LANES), F32)],
        scratch_shapes=[pltpu.VMEM((1, LANES), F32)],
        compiler_params=_params(2),
        name="fox_in",
    )(x, mod, gpre, w_main, w_vt, w_f3, b_f3, _tri(tt), hsel)


def _cumsum_kernel(x_ref, tri_ref, o_ref, car):
    @pl.when(pl.program_id(1) == 0)
    def _():
        car[...] = jnp.zeros_like(car)

    c = _tri_cumsum(tri_ref[...], x_ref[0], car[...])
    o_ref[0] = c
    tc = c.shape[0]
    car[...] = c[tc - 1:tc, :]


def _cumsum_time(x, *, tc):
    b, t, nh = x.shape
    return pl.pallas_call(
        _cumsum_kernel,
        grid=(b, t // tc),
        in_specs=[pl.BlockSpec((1, tc, nh), lambda i, j: (i, j, 0)), _const_spec((tc, tc))],
        out_specs=pl.BlockSpec((1, tc, nh), lambda i, j: (i, j, 0)),
        out_shape=jax.ShapeDtypeStruct((b, t, nh), F32),
        scratch_shapes=[pltpu.VMEM((1, nh), F32)],
        compiler_params=_params(2),
        name="cumsum_time",
    )(x, _tri(tc))


def _split_heads(q):
    qf = q.astype(F32)
    lane = lax.broadcasted_iota(jnp.int32, qf.shape, 1)
    return (jnp.where(lane < HEAD_DIM, qf, 0.0).astype(BF16),
            jnp.where(lane >= HEAD_DIM, qf, 0.0).astype(BF16))


def _attn_kernel(cnt_ref, q_ref, k_ref, a_ref, vt_ref, o_ref, z0, z1, zm0, zm1, p0, p1, al0, al1,
                 qa_s, m_s, acc_s, *, tq, tk, nh, tps):
    bi = pl.program_id(0)
    hp = pl.program_id(1)
    qs = pl.program_id(2)
    nq = pl.num_programs(2) * tps
    ns = 2 * tps
    z_s, zmax_s, p_s, alpha_s = (z0, z1), (zm0, zm1), (p0, p1), (al0, al1)

    @pl.when((bi == 0) & (hp == 0) & (qs == 0))
    def _():
        for p_ref, al_ref in zip(p_s, alpha_s):
            p_ref[...] = jnp.zeros_like(p_ref)
            al_ref[...] = jnp.ones_like(al_ref)

    lane = lax.broadcasted_iota(jnp.int32, (tq, LANES), 1)
    starts = [0]
    for g in range(ns):
        tile, hh = divmod(g, 2)
        head = 2 * hp + hh
        q = q_ref[0, tile * tq:(tile + 1) * tq, :].astype(F32)
        own = (lane >= hh * HEAD_DIM) & (lane < (hh + 1) * HEAD_DIM)
        pick = (lane == head) | (lane == head + nh) | (lane == head + 2 * nh)
        qa_s[g] = jnp.concatenate(
            [jnp.where(own, q, 0.0).astype(BF16), jnp.where(pick, 1.0, 0.0).astype(BF16)], axis=1)
        starts.append(starts[-1] + cnt_ref[(bi * nh + head) * nq + qs * tps + tile])
    n_off = starts[ns]
    last = ns + n_off - 1
    m_s[...] = jnp.full_like(m_s, NEG_BIG)
    acc_s[...] = jnp.zeros_like(acc_s)
    tv = vt_ref.shape[4]
    nsub = tk // tv

    def item(t):
        t = jnp.maximum(t, 0)
        u = t - ns
        g_off = sum((u >= starts[j]).astype(jnp.int32) for j in range(1, ns))
        start = sum(jnp.where(g_off == j, starts[j], 0) for j in range(1, ns))
        g = jnp.where(u < 0, t, g_off)
        back = jnp.where(u < 0, 0, u - start + 1)
        return g, g % 2, qs * tps + g // 2 - back

    def qk(t, slot, diagonal_stream=None):
        if diagonal_stream is None:
            g, _, blk = item(t)
        else:
            g, blk = diagonal_stream, qs * tps + diagonal_stream // 2
        st = pl.multiple_of(blk * tk, tk)
        keys = jnp.concatenate([k_ref[0, pl.ds(st, tk), :], a_ref[0, pl.ds(st, tk), :]], axis=1)
        z = lax.dot_general(keys, qa_s[g], _NT, preferred_element_type=F32)
        if diagonal_stream is not None:
            kpos = lax.broadcasted_iota(jnp.int32, z.shape, 0)
            qpos = lax.broadcasted_iota(jnp.int32, z.shape, 1)
            z = jnp.where(kpos <= qpos, z, NEG_BIG)
        z_s[slot][...] = z
        zmax_s[slot][...] = jnp.max(z, axis=0, keepdims=True)

    def softmax(t, slot):
        g, _, _ = item(t)
        m_prev = m_s[g]
        m_next = jnp.maximum(m_prev, zmax_s[slot][...])
        p_s[slot][...] = jnp.exp2(z_s[slot][...] - m_next).astype(BF16)
        alpha_s[slot][...] = jnp.exp2(m_prev - m_next)
        m_s[g] = m_next

    def pv(t, slot, valid):
        g, hh, blk = item(t)
        r = None
        for u in range(nsub):
            part = jnp.dot(vt_ref[0, hh, blk * nsub + u], p_s[slot][u * tv:(u + 1) * tv, :],
                           preferred_element_type=F32)
            r = part if r is None else r + part
        if valid is not None:
            r = jnp.where(valid, r, 0.0)
        acc_s[g] = acc_s[g] * alpha_s[slot][...] + r

    def iteration(t, slot, next_diagonal_stream=None):
        if not (isinstance(t, int) and t < 2):
            pv(t - 2, slot, None if isinstance(t, int) else t >= 2)
        softmax(t, slot)
        qk(t + 2, slot, next_diagonal_stream)

    def finish(slot):
        pv(last - 3, slot, last >= 3)
        softmax(last - 1, slot)
        pv(last - 2, 1 - slot, last >= 2)
        softmax(last, 1 - slot)
        pv(last - 1, slot, None)
        pv(last, 1 - slot, None)
        for tile in range(tps):
            outs = []
            for hh in range(2):
                acc = acc_s[2 * tile + hh]
                outs.append(acc[:HEAD_DIM] / acc[HEAD_DIM:HEAD_DIM + 1])
            o_ref[0, tile * tq:(tile + 1) * tq, :] = (
                jnp.concatenate(outs, axis=0).T.astype(o_ref.dtype))

    qk(0, 0, 0)
    qk(1, 1, 1)
    for t in range(ns - 2):
        iteration(t, t % 2, t + 2)
    first = ns - 2

    def pair(k, carry):
        iteration(first + 2 * k, 0)
        iteration(first + 2 * k + 1, 1)
        return carry

    lax.fori_loop(0, n_off // 2, pair, 0)
    odd = lax.rem(n_off, 2) == 1

    @pl.when(odd)
    def _():
        iteration(first + n_off - 1, 0)
        finish(1)

    @pl.when(jnp.logical_not(odd))
    def _():
        finish(0)


def _attention(cnt, qb, kb, aug, vt, nh, *, tq, tk):
    assert tq == tk, "one key block per query tile on the diagonal"
    b, t, w = qb.shape
    hp = w // LANES
    nq = t // tq
    tps = next(n for n in (4, 2, 1) if nq % n == 0)
    ns = 2 * tps
    _, _, nv, rows, tv = vt.shape
    kern = functools.partial(_attn_kernel, tq=tq, tk=tk, nh=nh, tps=tps)
    grid_spec = pltpu.PrefetchScalarGridSpec(
        num_scalar_prefetch=1,
        grid=(b, hp, nq // tps),
        in_specs=[
            pl.BlockSpec((1, tps * tq, LANES), lambda i, h, j, c: (i, j, h)),
            pl.BlockSpec((1, t, LANES), lambda i, h, j, c: (i, 0, h)),
            pl.BlockSpec((1, t, LANES), lambda i, h, j, c: (i, 0, 0)),
            pl.BlockSpec((1, 2, nv, rows, tv), lambda i, h, j, c: (i, h, 0, 0, 0)),
        ],
        out_specs=pl.BlockSpec((1, tps * tq, LANES), lambda i, h, j, c: (i, j, h)),
        scratch_shapes=(
            [pltpu.VMEM((tk, tq), F32)] * 2
            + [pltpu.VMEM((1, tq), F32)] * 2
            + [pltpu.VMEM((tk, tq), BF16)] * 2
            + [pltpu.VMEM((1, tq), F32)] * 2
            + [pltpu.VMEM((ns, tq, 2 * LANES), BF16),
               pltpu.VMEM((ns, 1, tq), F32),
               pltpu.VMEM((ns, rows, tq), F32)]))
    return pl.pallas_call(
        kern,
        grid_spec=grid_spec,
        out_shape=jax.ShapeDtypeStruct((b, t, w), BF16),
        compiler_params=_params(3),
        name="fox_attention",
    )(cnt, qb, kb, aug, vt)


EXP2_ZERO_BELOW = 150.0


def _plan_kernel(qn_ref, kn_ref, cend_ref, cnt_ref, *, nh, ndiag):
    qn2 = jnp.max(qn_ref[0], axis=1)[:, :nh]
    kn2 = jnp.max(kn_ref[0], axis=0, keepdims=True)[:, :nh]
    bend = cend_ref[0] * (-LOG2E)
    nk = bend.shape[0]
    reach = 2.04 * jnp.sqrt(qn2 * kn2)
    blk = lax.broadcasted_iota(jnp.int32, (nk, nh), 0)
    rows = []
    for qi in range(qn2.shape[0]):
        d0 = qi * ndiag
        bprev = bend[d0 - 1:d0] if d0 > 0 else jnp.zeros((1, nh), F32)
        slack = 2.0 + 1e-5 * (jnp.abs(bprev) + jnp.abs(bend))
        live = (blk < d0) & (bprev - bend <= EXP2_ZERO_BELOW + reach[qi:qi + 1] + slack)
        rows.append(jnp.max(jnp.where(live, d0 - blk, 0), axis=0, keepdims=True))
    cnt_ref[0] = jnp.concatenate(rows, axis=0)


def _plan(qn2, kn2, cend, nh):
    b, nq, sub, _ = qn2.shape
    nt = kn2.shape[1]
    nk = cend.shape[1]
    return pl.pallas_call(
        functools.partial(_plan_kernel, nh=nh, ndiag=nk // nq),
        grid=(b,),
        in_specs=[
            pl.BlockSpec((1, nq, sub, LANES), lambda i: (i, 0, 0, 0)),
            pl.BlockSpec((1, nt, LANES), lambda i: (i, 0, 0)),
            pl.BlockSpec((1, nk, nh), lambda i: (i, 0, 0)),
        ],
        out_specs=pl.BlockSpec((1, nq, nh), lambda i: (i, 0, 0)),
        out_shape=jax.ShapeDtypeStruct((b, nq, nh), jnp.int32),
        compiler_params=_params(1),
        name="fox_plan",
    )(qn2, kn2, cend)


def _attn_cached_kernel(q_ref, kn_ref, vn_ref, pk_ref, pv_ref, cp_ref, cn_ref, o_ref):
    qh = _split_heads(q_ref[0])
    pkb = pk_ref[0].astype(BF16)
    pvb = pv_ref[0].astype(BF16)
    kn = kn_ref[0]
    vn = vn_ref[0]
    t = kn.shape[0]
    p_len = pkb.shape[0]
    rowi = lax.broadcasted_iota(jnp.int32, (t, t), 0)
    coli = lax.broadcasted_iota(jnp.int32, (t, t), 1)
    outs = []
    for hh in range(2):
        cp = cp_ref[0, 0, hh:hh + 1, :]
        tot = cp[:, p_len - 1:p_len]
        z1 = lax.dot_general(qh[hh], pkb, _NT, preferred_element_type=F32) - (cp - tot) * LOG2E
        z2 = (lax.dot_general(qh[hh], kn, _NT, preferred_element_type=F32)
              - cn_ref[0, 0, hh:hh + 1, :] * LOG2E)
        z2 = jnp.where(coli <= rowi, z2, NEG_BIG)
        m = jnp.maximum(jnp.max(z1, axis=1), jnp.max(z2, axis=1))[:, None]
        p1 = jnp.exp2(z1 - m)
        p2 = jnp.exp2(z2 - m)
        l = (jnp.sum(p1, axis=1) + jnp.sum(p2, axis=1))[:, None]
        o = (jnp.dot(p1.astype(BF16), pvb, preferred_element_type=F32)
             + jnp.dot(p2.astype(BF16), vn, preferred_element_type=F32))
        outs.append(o / l)
    lane = lax.broadcasted_iota(jnp.int32, (t, LANES), 1)
    o_ref[0] = jnp.where(lane < HEAD_DIM, outs[0], outs[1]).astype(o_ref.dtype)


def _attention_cached(qb, kb, vb, past_k, past_v, cum_past, cum_new):
    b, t, w = qb.shape
    p_len = past_k.shape[1]
    hp = w // LANES
    new_spec = pl.BlockSpec((1, t, LANES), lambda i, h: (i, 0, h))
    past_spec = pl.BlockSpec((1, p_len, LANES), lambda i, h: (i, 0, h))
    return pl.pallas_call(
        _attn_cached_kernel,
        grid=(b, hp),
        in_specs=[new_spec, new_spec, new_spec, past_spec, past_spec,
                  pl.BlockSpec((1, 1, 2, p_len), lambda i, h: (i, h, 0, 0)),
                  pl.BlockSpec((1, 1, 2, t), lambda i, h: (i, h, 0, 0))],
        out_specs=new_spec,
        out_shape=jax.ShapeDtypeStruct((b, t, w), BF16),
        compiler_params=_params(2),
        name="fox_attention_cached",
    )(qb, kb, vb, past_k, past_v,
      cum_past.reshape(b, hp, 2, p_len), cum_new.reshape(b, hp, 2, t))


def _fox_out_kernel(o_ref, sg_ref, x_ref, mod_ref, gpost_ref, w_ref, y_ref):
    yv = (o_ref[0].astype(F32) * sg_ref[0].astype(F32)).astype(BF16)
    y = jnp.dot(yv, w_ref[...], preferred_element_type=F32)
    y_ref[0] = _gated_residual(x_ref[0], y, mod_ref[0], gpost_ref[...])


def _fox_out(o, g, x, mod, mod_off, gpost, w_out, *, tt):
    b, t, d = x.shape
    w = o.shape[-1]
    return pl.pallas_call(
        _fox_out_kernel,
        grid=(b, t // tt),
        in_specs=[
            pl.BlockSpec((1, tt, w), lambda i, j: (i, j, 0)),
            pl.BlockSpec((1, tt, w), lambda i, j: (i, j, 0)),
            pl.BlockSpec((1, tt, d), lambda i, j: (i, j, 0)),
            pl.BlockSpec((1, 3, d), lambda i, j: (i + mod_off, 0, 0)),
            _const_spec((1, d)),
            _const_spec((w, d)),
        ],
        out_specs=pl.BlockSpec((1, tt, d), lambda i, j: (i, j, 0)),
        out_shape=jax.ShapeDtypeStruct((b, t, d), F32),
        compiler_params=_params(2),
        name="fox_out",
    )(o, g, x, mod, gpost, w_out)


def _tile(t, pref):
    return pref if t % pref == 0 else t


def kernel(x_prompt, x_sample, c_prompt, c_sample, state_lru_h, state_lru_conv, cache_fox_k, cache_fox_v, cache_fox_logf, norm_pre, norm_post, ada_w, ada_b, lru_w_in, lru_conv_w, lru_conv_b, lru_w_a, lru_b_a, lru_w_x, lru_b_x, lru_lambda, lru_w_out, fox_w_in, fox_b_f, fox_w_out):
    b, t, d = x_prompt.shape
    db, dt, _ = x_sample.shape
    depth = ada_w.shape[0]
    r = lru_lambda.shape[-1]
    nh = fox_b_f.shape[-1]
    w = fox_w_out.shape[1]
    p_len = cache_fox_k.shape[2]
    scale = float(HEAD_DIM) ** -0.5

    c_all = jnp.concatenate([c_prompt, c_sample], axis=0)
    mod = _adaln(c_all, ada_w, ada_b).reshape(depth, b + db, 3, d)

    xp, xs = x_prompt, x_sample

    j = 0
    w_ax = jnp.concatenate([lru_w_a[j], lru_w_x[j]], axis=-1).astype(BF16)
    lru_args = (norm_pre[0:1], norm_post[0:1], lru_w_in[j].astype(BF16), lru_conv_w[j],
                lru_conv_b[j:j + 1], w_ax, lru_b_a[j:j + 1], lru_b_x[j:j + 1],
                lru_lambda[j:j + 1], lru_w_out[j].astype(BF16))
    xp, conv_p, h_p = _lru_layer(
        xp, mod[0], 0, *lru_args,
        jnp.zeros((b, CONV_TAPS - 1, r), F32), jnp.zeros((b, 1, r), F32), tt=_tile(t, 512))
    xs, conv_s, h_s = _lru_layer(
        xs, mod[0], b, *lru_args,
        state_lru_conv[j], state_lru_h[j].reshape(db, 1, r), tt=_tile(dt, 256))

    w_main = fox_w_in[j][:, :4 * w].astype(BF16)
    pad = jnp.zeros((d, LANES - 3 * nh), F32)
    w_f3 = jnp.concatenate([fox_w_in[j][:, 4 * w:]] * 3 + [pad], axis=1).astype(BF16)
    b_f3 = jnp.concatenate([fox_b_f[j]] * 3 + [jnp.zeros((LANES - 3 * nh,), F32)]).reshape(1, LANES)
    w_out = fox_w_out[j].astype(BF16)
    qscale = scale * LOG2E

    w_vt = fox_w_in[j][:, 2 * w:3 * w].T.astype(BF16)
    tt = _tile(t, 256)
    tk = _tile(t, 512)
    tq = tk
    qb, kb, k_p, v_p, sg_p, vt_p, lf_p, cum_p, aug_p, qn_p, kn_p = _fox_in(
        xp, mod[1], 0, norm_pre[1:2], w_main, w_vt, w_f3, b_f3, nh, tt=tt, qscale=qscale)
    cnt = _plan(qn_p.reshape(b, t // tq, tq // tt, LANES), kn_p.reshape(b, t // tt, LANES),
                cum_p[:, tk - 1::tk, :], nh)
    o_p = _attention(cnt.transpose(0, 2, 1).reshape(-1), qb, kb, aug_p, vt_p, nh, tq=tq, tk=tk)
    yp = _fox_out(o_p, sg_p, xp, mod[1], 0, norm_post[1:2], w_out, tt=_tile(t, 512))

    qs, ksb, k_s, v_s, sg_s, _, lf_s, cum_s, _, _, _ = _fox_in(
        xs, mod[1], b, norm_pre[1:2], w_main, w_vt, w_f3, b_f3, nh, tt=_tile(dt, 256), qscale=qscale)
    lf_past = cache_fox_logf[j].transpose(1, 0, 2).reshape(1, p_len, db * nh)
    cum_past = _cumsum_time(lf_past, tc=_tile(p_len, 512)).reshape(p_len, db, nh).transpose(1, 2, 0)
    o_s = _attention_cached(qs, ksb, v_s.astype(BF16),
                            cache_fox_k[j].reshape(db, p_len, w),
                            cache_fox_v[j].reshape(db, p_len, w),
                            cum_past, jnp.swapaxes(cum_s, 1, 2))
    ys = _fox_out(o_s, sg_s, xs, mod[1], b, norm_post[1:2], w_out, tt=_tile(dt, 512))

    hd = (nh, HEAD_DIM)
    return (yp, ys,
            h_p.reshape(1, b, r), conv_p[None],
            k_p.reshape(1, b, t, *hd), v_p.reshape(1, b, t, *hd), lf_p[None],
            h_s.reshape(1, db, r), conv_s[None],
            k_s.reshape(1, db, dt, *hd), v_s.reshape(1, db, dt, *hd), lf_s[None])
```

```python
import functools

import jax
import jax.numpy as jnp
from jax import lax
from jax.experimental import pallas as pl
from jax.experimental.pallas import tpu as pltpu

F32 = jnp.float32
BF16 = jnp.bfloat16

EPS = 1e-6
LRU_C = 8.0
CONV_TAPS = 4
HEAD_DIM = 64
LANES = 128
SUBLANES = 8
NEG_BIG = -1e30
LOG2E = 1.4426950408889634
ONES_ROWS = 16
VMEM_LIMIT_BYTES = 56 * 1024 * 1024

_NT = (((1,), (1,)), ((), ()))


def _params(n_grid):
    return pltpu.CompilerParams(
        dimension_semantics=("arbitrary",) * n_grid,
        vmem_limit_bytes=VMEM_LIMIT_BYTES)


def _const_spec(shape):
    nd = len(shape)
    return pl.BlockSpec(shape, lambda *_: (0,) * nd, pipeline_mode=pl.Buffered(1))


def _sigmoid(x):
    return 0.5 * jnp.tanh(0.5 * x) + 0.5


def _silu(x):
    return x * _sigmoid(x)


def _softplus(x):
    return jnp.maximum(x, 0.0) + jnp.log1p(jnp.exp(-jnp.abs(x)))


def _one_minus_exp_neg(t, exp_neg_t):
    p = jnp.full_like(t, -1.0 / 24.0)
    for c in (1.0 / 6.0, -0.5, 1.0):
        p = p * t + c
    return jnp.where(t < 1.0 / 64.0, t * p, 1.0 - exp_neg_t)


def _sqrt_pos(x):
    x = jnp.maximum(x, 1e-30)
    return x * lax.rsqrt(x)


def _modulated_norm(x, mod, g):
    ms = jnp.mean(x * x, axis=-1, keepdims=True)
    h = x * lax.rsqrt(ms + EPS) * g
    return h * (1.0 + mod[1:2]) + mod[0:1]


def _gated_residual(x, y, mod, g):
    ms = jnp.mean(y * y, axis=-1, keepdims=True)
    return x + mod[2:3] * (y * lax.rsqrt(ms + EPS) * g)


def _adaln_kernel(c_ref, w_ref, b_ref, o_ref):
    sc = _silu(c_ref[...]).astype(BF16)
    o_ref[0] = jnp.dot(sc, w_ref[0].astype(BF16), preferred_element_type=F32) + b_ref[0]


def _adaln(c_all, ada_w, ada_b):
    depth, d, d3 = ada_w.shape
    n = c_all.shape[0]
    nj = d3 // d
    return pl.pallas_call(
        _adaln_kernel,
        grid=(depth, nj),
        in_specs=[
            pl.BlockSpec((n, d), lambda i, j: (0, 0)),
            pl.BlockSpec((1, d, d), lambda i, j: (i, 0, j)),
            pl.BlockSpec((1, 1, d), lambda i, j: (i, 0, j)),
        ],
        out_specs=pl.BlockSpec((1, n, d), lambda i, j: (i, 0, j)),
        out_shape=jax.ShapeDtypeStruct((depth, n, d3), F32),
        compiler_params=_params(2),
        name="adaln",
    )(c_all, ada_w, ada_b.reshape(depth, 1, d3))


def _lru_kernel(x_ref, mod_ref, gpre_ref, gpost_ref, win_ref, cw_ref, cb_ref, wax_ref,
                ba_ref, bx_ref, lam_ref, wout_ref, conv0_ref, h0_ref,
                y_ref, convo_ref, hlast_ref,
                xbuf, xc_s, pre_a, pre_x, gate_s, yv_s, hcar, *, tt, r, nblk):
    t = pl.program_id(1)
    nt = pl.num_programs(1)
    tail0 = SUBLANES - (CONV_TAPS - 1)

    @pl.when(t == 0)
    def _():
        xbuf[tail0:SUBLANES, :] = conv0_ref[0]
        hcar[...] = h0_ref[0]

    mod = mod_ref[0]
    cw = cw_ref[...]
    half = tt // 2

    def front(lo):
        h = _modulated_norm(x_ref[0, lo:lo + half, :], mod, gpre_ref[...])
        z = jnp.dot(h.astype(BF16), win_ref[...], preferred_element_type=F32)
        xbuf[SUBLANES + lo:SUBLANES + lo + half, :] = z[:, :r]
        gate_s[lo:lo + half, :] = z[:, r:]
        xc = cb_ref[...] + xbuf[tail0 + lo:tail0 + lo + half, :] * cw[0:1]
        for k in range(1, CONV_TAPS):
            xc = xc + xbuf[tail0 + k + lo:tail0 + k + lo + half, :] * cw[k:k + 1]
        xc_s[lo:lo + half, :] = xc
        xcb = xc.astype(BF16)
        for n in range(nblk):
            c0, c1 = n * LANES, (n + 1) * LANES
            res = jnp.dot(xcb[:, c0:c1], wax_ref[n], preferred_element_type=F32)
            pre_a[lo:lo + half, c0:c1] = res[:, :LANES]
            pre_x[lo:lo + half, c0:c1] = res[:, LANES:]

    sp = LRU_C * _softplus(-lam_ref[...])
    c1 = (-0.5 * LOG2E) * sp
    c2 = sp
    ba = ba_ref[...]
    bx = bx_ref[...]
    row = lax.broadcasted_iota(jnp.int32, (SUBLANES, r), 0)

    def group(st, hprev):
        sl = slice(st, st + SUBLANES)
        th_a = jnp.tanh(pre_a[sl, :] + ba)
        th_x = jnp.tanh(pre_x[sl, :] + bx)
        a = jnp.exp2(c1 * th_a + c1)
        ig = 0.5 * th_x + 0.5
        u = _sqrt_pos(_one_minus_exp_neg(c2 * th_a + c2, a * a)) * (ig * xc_s[sl, :])
        for s in (1, 2, 4):
            a_sh = jnp.where(row >= s, pltpu.roll(a, s, axis=0), 1.0)
            u_sh = jnp.where(row >= s, pltpu.roll(u, s, axis=0), 0.0)
            u = a * u_sh + u
            a = a * a_sh
        hs = a * hprev + u
        gh = gate_s[sl, :]
        yv_s[sl, :] = hs * (gh * jnp.tanh(gh) + gh)
        return hs[SUBLANES - 1:SUBLANES, :]

    def scan(lo, hprev):
        for st in range(lo, lo + half, SUBLANES):
            hprev = group(st, hprev)
        return hprev

    def back(lo):
        y = jnp.dot(yv_s[lo:lo + half, :].astype(BF16), wout_ref[...], preferred_element_type=F32)
        y_ref[0, lo:lo + half, :] = _gated_residual(x_ref[0, lo:lo + half, :], y, mod, gpost_ref[...])

    front(0)
    front(half)
    tail = xbuf[tt + tail0:tt + SUBLANES, :]
    hmid = scan(0, hcar[...])
    back(0)
    hlast = scan(half, hmid)
    back(half)
    xbuf[tail0:SUBLANES, :] = tail
    hcar[...] = hlast

    @pl.when(t == nt - 1)
    def _():
        convo_ref[0] = tail
        hlast_ref[0] = hlast


def _lru_layer(x, mod, mod_off, gpre, gpost, w_in, conv_w, conv_b, w_ax, b_a, b_x, lam, w_out,
               conv0, h0, *, tt):
    b, t, d = x.shape
    r = lam.shape[-1]
    nblk = w_ax.shape[0]
    kern = functools.partial(_lru_kernel, tt=tt, r=r, nblk=nblk)
    return pl.pallas_call(
        kern,
        grid=(b, t // tt),
        in_specs=[
            pl.BlockSpec((1, tt, d), lambda i, j: (i, j, 0)),
            pl.BlockSpec((1, 3, d), lambda i, j: (i + mod_off, 0, 0)),
            _const_spec((1, d)), _const_spec((1, d)),
            _const_spec((d, 2 * r)),
            _const_spec((CONV_TAPS, r)), _const_spec((1, r)),
            _const_spec((nblk, LANES, 2 * LANES)),
            _const_spec((1, r)), _const_spec((1, r)), _const_spec((1, r)),
            _const_spec((r, d)),
            pl.BlockSpec((1, CONV_TAPS - 1, r), lambda i, j: (i, 0, 0)),
            pl.BlockSpec((1, 1, r), lambda i, j: (i, 0, 0)),
        ],
        out_specs=[
            pl.BlockSpec((1, tt, d), lambda i, j: (i, j, 0)),
            pl.BlockSpec((1, CONV_TAPS - 1, r), lambda i, j: (i, 0, 0)),
            pl.BlockSpec((1, 1, r), lambda i, j: (i, 0, 0)),
        ],
        out_shape=[
            jax.ShapeDtypeStruct((b, t, d), F32),
            jax.ShapeDtypeStruct((b, CONV_TAPS - 1, r), F32),
            jax.ShapeDtypeStruct((b, 1, r), F32),
        ],
        scratch_shapes=[
            pltpu.VMEM((tt + SUBLANES, r), F32),
            pltpu.VMEM((tt, r), F32), pltpu.VMEM((tt, r), F32), pltpu.VMEM((tt, r), F32),
            pltpu.VMEM((tt, r), F32), pltpu.VMEM((tt, r), F32),
            pltpu.VMEM((1, r), F32),
        ],
        compiler_params=_params(2),
        name="lru_layer",
    )(x, mod, gpre, gpost, w_in, conv_w, conv_b, w_ax, b_a, b_x, lam, w_out, conv0, h0)


def _split3(x):
    x1 = x.astype(BF16)
    r1 = x - x1.astype(F32)
    x2 = r1.astype(BF16)
    x3 = (r1 - x2.astype(F32)).astype(BF16)
    return x1, x2, x3


def _tri_cumsum(tri, x, carry):
    c = carry
    for piece in _split3(x):
        c = c + jnp.dot(tri, piece, preferred_element_type=F32)
    return c


def _fox_in_kernel(x_ref, mod_ref, gpre_ref, w_ref, wf_ref, bf_ref, tri_ref, hsel_ref,
                   q_ref, kb_ref, k_ref, v_ref, sg_ref, vt_ref, lf_ref, cum_ref, aug_ref,
                   qn_ref, kn_ref, car, *, w, nh, qscale):
    @pl.when(pl.program_id(1) == 0)
    def _():
        car[...] = jnp.zeros_like(car)

    hb = _modulated_norm(x_ref[0], mod_ref[0], gpre_ref[...]).astype(BF16)
    z = jnp.dot(hb, w_ref[...], preferred_element_type=F32)
    q = z[:, :w] * qscale
    q_ref[0] = q.astype(BF16)
    k = z[:, w:2 * w]
    k_ref[0] = k
    kb_ref[0] = k.astype(BF16)
    hsel = hsel_ref[...]
    qn_ref[0, 0] = jnp.max(jnp.dot((q * q).astype(BF16), hsel, preferred_element_type=F32),
                           axis=0, keepdims=True)
    kn_ref[0, 0] = jnp.max(jnp.dot((k * k).astype(BF16), hsel, preferred_element_type=F32),
                           axis=0, keepdims=True)
    v = z[:, 2 * w:3 * w]
    v_ref[0] = v
    sg_ref[0] = _silu(z[:, 3 * w:]).astype(BF16)
    tt = v.shape[0]
    vt = v.T.reshape(nh, HEAD_DIM, tt)
    vt_ref[0, :, 0] = jnp.concatenate([vt, jnp.ones((nh, ONES_ROWS, tt), F32)], axis=1).astype(BF16)
    fl = jnp.dot(hb, wf_ref[...], preferred_element_type=F32) + bf_ref[...]
    lf = -_softplus(-fl)
    c = _tri_cumsum(tri_ref[...], lf, car[...])
    tt = c.shape[0]
    car[...] = c[tt - 1:tt, :]
    lf_ref[0] = lf[:, :nh]
    cum_ref[0] = c[:, :nh]
    y1, y2, y3 = _split3(c * (-LOG2E))
    lane = lax.broadcasted_iota(jnp.int32, c.shape, 1)
    aug = jnp.where(lane < nh, y1.astype(F32),
                    jnp.where(lane < 2 * nh, y2.astype(F32),
                              jnp.where(lane < 3 * nh, y3.astype(F32), 0.0)))
    aug_ref[0] = aug.astype(BF16)


def _tri(n):
    return jnp.tril(jnp.ones((n, n), F32)).astype(BF16)


def _fox_in(x, mod, mod_off, gpre, w_main, w_f3, b_f3, nh, *, tt, qscale):
    b, t, d = x.shape
    w = w_main.shape[1] // 4
    rows = HEAD_DIM + ONES_ROWS
    kern = functools.partial(_fox_in_kernel, w=w, nh=nh, qscale=qscale)
    tile = lambda dt: jax.ShapeDtypeStruct((b, t, w), dt)
    tile_spec = pl.BlockSpec((1, tt, w), lambda i, j: (i, j, 0))
    head_spec = pl.BlockSpec((1, tt, nh), lambda i, j: (i, j, 0))
    norm_spec = pl.BlockSpec((1, 1, 1, LANES), lambda i, j: (i, j, 0, 0))
    hsel = (jnp.arange(w)[:, None] // HEAD_DIM == jnp.arange(LANES)[None, :]).astype(BF16)
    return pl.pallas_call(
        kern,
        grid=(b, t // tt),
        in_specs=[
            pl.BlockSpec((1, tt, d), lambda i, j: (i, j, 0)),
            pl.BlockSpec((1, 3, d), lambda i, j: (i + mod_off, 0, 0)),
            _const_spec((1, d)),
            _const_spec((d, 4 * w)),
            _const_spec((d, LANES)),
            _const_spec((1, LANES)),
            _const_spec((tt, tt)),
            _const_spec((w, LANES)),
        ],
        out_specs=[tile_spec] * 5 + [pl.BlockSpec((1, nh, 1, rows, tt), lambda i, j: (i, 0, j, 0, 0)),
                                     head_spec, head_spec,
                                     pl.BlockSpec((1, tt, LANES), lambda i, j: (i, j, 0)),
                                     norm_spec, norm_spec],
        out_shape=[tile(BF16), tile(BF16), tile(F32), tile(F32), tile(BF16),
                   jax.ShapeDtypeStruct((b, nh, t // tt, rows, tt), BF16),
                   jax.ShapeDtypeStruct((b, t, nh), F32), jax.ShapeDtypeStruct((b, t, nh), F32),
                   jax.ShapeDtypeStruct((b, t, LANES), BF16),
                   jax.ShapeDtypeStruct((b, t // tt, 1, LANES), F32),
                   jax.ShapeDtypeStruct((b, t // tt, 1, LANES), F32)],
        scratch_shapes=[pltpu.VMEM((1, LANES), F32)],
        compiler_params=_params(2),
        name="fox_in",
    )(x, mod, gpre, w_main, w_f3, b_f3, _tri(tt), hsel)


def _cumsum_kernel(x_ref, tri_ref, o_ref, car):
    @pl.when(pl.program_id(1) == 0)
    def _():
        car[...] = jnp.zeros_like(car)

    c = _tri_cumsum(tri_ref[...], x_ref[0], car[...])
    o_ref[0] = c
    tc = c.shape[0]
    car[...] = c[tc - 1:tc, :]


def _cumsum_time(x, *, tc):
    b, t, nh = x.shape
    return pl.pallas_call(
        _cumsum_kernel,
        grid=(b, t // tc),
        in_specs=[pl.BlockSpec((1, tc, nh), lambda i, j: (i, j, 0)), _const_spec((tc, tc))],
        out_specs=pl.BlockSpec((1, tc, nh), lambda i, j: (i, j, 0)),
        out_shape=jax.ShapeDtypeStruct((b, t, nh), F32),
        scratch_shapes=[pltpu.VMEM((1, nh), F32)],
        compiler_params=_params(2),
        name="cumsum_time",
    )(x, _tri(tc))


def _split_heads(q):
    qf = q.astype(F32)
    lane = lax.broadcasted_iota(jnp.int32, qf.shape, 1)
    return (jnp.where(lane < HEAD_DIM, qf, 0.0).astype(BF16),
            jnp.where(lane >= HEAD_DIM, qf, 0.0).astype(BF16))


def _attn_kernel(cnt_ref, q_ref, k_ref, a_ref, vt_ref, o_ref, z0, z1, zm0, zm1, p0, p1, al0, al1,
                 qa_s, m_s, acc_s, *, tq, tk, nh, tps):
    bi = pl.program_id(0)
    hp = pl.program_id(1)
    qs = pl.program_id(2)
    nq = pl.num_programs(2) * tps
    ns = 2 * tps
    z_s, zmax_s, p_s, alpha_s = (z0, z1), (zm0, zm1), (p0, p1), (al0, al1)

    @pl.when((bi == 0) & (hp == 0) & (qs == 0))
    def _():
        for p_ref, al_ref in zip(p_s, alpha_s):
            p_ref[...] = jnp.zeros_like(p_ref)
            al_ref[...] = jnp.ones_like(al_ref)

    lane = lax.broadcasted_iota(jnp.int32, (tq, LANES), 1)
    starts = [0]
    for g in range(ns):
        tile, hh = divmod(g, 2)
        head = 2 * hp + hh
        q = q_ref[0, tile * tq:(tile + 1) * tq, :].astype(F32)
        own = (lane >= hh * HEAD_DIM) & (lane < (hh + 1) * HEAD_DIM)
        pick = (lane == head) | (lane == head + nh) | (lane == head + 2 * nh)
        qa_s[g] = jnp.concatenate(
            [jnp.where(own, q, 0.0).astype(BF16), jnp.where(pick, 1.0, 0.0).astype(BF16)], axis=1)
        starts.append(starts[-1] + cnt_ref[(bi * nh + head) * nq + qs * tps + tile])
    n_off = starts[ns]
    last = ns + n_off - 1
    m_s[...] = jnp.full_like(m_s, NEG_BIG)
    acc_s[...] = jnp.zeros_like(acc_s)
    tv = vt_ref.shape[4]
    nsub = tk // tv

    def item(t):
        t = jnp.maximum(t, 0)
        u = t - ns
        g_off = sum((u >= starts[j]).astype(jnp.int32) for j in range(1, ns))
        start = sum(jnp.where(g_off == j, starts[j], 0) for j in range(1, ns))
        g = jnp.where(u < 0, t, g_off)
        back = jnp.where(u < 0, 0, u - start + 1)
        return g, g % 2, qs * tps + g // 2 - back

    def qk(t, slot, diagonal_stream=None):
        if diagonal_stream is None:
            g, _, blk = item(t)
        else:
            g, blk = diagonal_stream, qs * tps + diagonal_stream // 2
        st = pl.multiple_of(blk * tk, tk)
        keys = jnp.concatenate([k_ref[0, pl.ds(st, tk), :], a_ref[0, pl.ds(st, tk), :]], axis=1)
        z = lax.dot_general(keys, qa_s[g], _NT, preferred_element_type=F32)
        if diagonal_stream is not None:
            kpos = lax.broadcasted_iota(jnp.int32, z.shape, 0)
            qpos = lax.broadcasted_iota(jnp.int32, z.shape, 1)
            z = jnp.where(kpos <= qpos, z, NEG_BIG)
        z_s[slot][...] = z
        zmax_s[slot][...] = jnp.max(z, axis=0, keepdims=True)

    def softmax(t, slot):
        g, _, _ = item(t)
        m_prev = m_s[g]
        m_next = jnp.maximum(m_prev, zmax_s[slot][...])
        p_s[slot][...] = jnp.exp2(z_s[slot][...] - m_next).astype(BF16)
        alpha_s[slot][...] = jnp.exp2(m_prev - m_next)
        m_s[g] = m_next

    def pv(t, slot, valid):
        g, hh, blk = item(t)
        r = None
        for u in range(nsub):
            part = jnp.dot(vt_ref[0, hh, blk * nsub + u], p_s[slot][u * tv:(u + 1) * tv, :],
                           preferred_element_type=F32)
            r = part if r is None else r + part
        if valid is not None:
            r = jnp.where(valid, r, 0.0)
        acc_s[g] = acc_s[g] * alpha_s[slot][...] + r

    def iteration(t, slot, next_diagonal_stream=None):
        if not (isinstance(t, int) and t < 2):
            pv(t - 2, slot, None if isinstance(t, int) else t >= 2)
        softmax(t, slot)
        qk(t + 2, slot, next_diagonal_stream)

    def finish(slot):
        pv(last - 3, slot, last >= 3)
        softmax(last - 1, slot)
        pv(last - 2, 1 - slot, last >= 2)
        softmax(last, 1 - slot)
        pv(last - 1, slot, None)
        pv(last, 1 - slot, None)
        for tile in range(tps):
            outs = []
            for hh in range(2):
                acc = acc_s[2 * tile + hh]
                outs.append(acc[:HEAD_DIM] / acc[HEAD_DIM:HEAD_DIM + 1])
            o_ref[0, tile * tq:(tile + 1) * tq, :] = (
                jnp.concatenate(outs, axis=0).T.astype(o_ref.dtype))

    qk(0, 0, 0)
    qk(1, 1, 1)
    for t in range(ns - 2):
        iteration(t, t % 2, t + 2)
    first = ns - 2

    def pair(k, carry):
        iteration(first + 2 * k, 0)
        iteration(first + 2 * k + 1, 1)
        return carry

    lax.fori_loop(0, n_off // 2, pair, 0)
    odd = lax.rem(n_off, 2) == 1

    @pl.when(odd)
    def _():
        iteration(first + n_off - 1, 0)
        finish(1)

    @pl.when(jnp.logical_not(odd))
    def _():
        finish(0)


def _attention(cnt, qb, kb, aug, vt, nh, *, tq, tk):
    assert tq == tk, "one key block per query tile on the diagonal"
    b, t, w = qb.shape
    hp = w // LANES
    nq = t // tq
    tps = next(n for n in (4, 2, 1) if nq % n == 0)
    ns = 2 * tps
    _, _, nv, rows, tv = vt.shape
    kern = functools.partial(_attn_kernel, tq=tq, tk=tk, nh=nh, tps=tps)
    grid_spec = pltpu.PrefetchScalarGridSpec(
        num_scalar_prefetch=1,
        grid=(b, hp, nq // tps),
        in_specs=[
            pl.BlockSpec((1, tps * tq, LANES), lambda i, h, j, c: (i, j, h)),
            pl.BlockSpec((1, t, LANES), lambda i, h, j, c: (i, 0, h)),
            pl.BlockSpec((1, t, LANES), lambda i, h, j, c: (i, 0, 0)),
            pl.BlockSpec((1, 2, nv, rows, tv), lambda i, h, j, c: (i, h, 0, 0, 0)),
        ],
        out_specs=pl.BlockSpec((1, tps * tq, LANES), lambda i, h, j, c: (i, j, h)),
        scratch_shapes=(
            [pltpu.VMEM((tk, tq), F32)] * 2
            + [pltpu.VMEM((1, tq), F32)] * 2
            + [pltpu.VMEM((tk, tq), BF16)] * 2
            + [pltpu.VMEM((1, tq), F32)] * 2
            + [pltpu.VMEM((ns, tq, 2 * LANES), BF16),
               pltpu.VMEM((ns, 1, tq), F32),
               pltpu.VMEM((ns, rows, tq), F32)]))
    return pl.pallas_call(
        kern,
        grid_spec=grid_spec,
        out_shape=jax.ShapeDtypeStruct((b, t, w), BF16),
        compiler_params=_params(3),
        name="fox_attention",
    )(cnt, qb, kb, aug, vt)


EXP2_ZERO_BELOW = 150.0


def _plan_kernel(qn_ref, kn_ref, cend_ref, cnt_ref, *, nh, ndiag):
    qn2 = jnp.max(qn_ref[0], axis=1)[:, :nh]
    kn2 = jnp.max(kn_ref[0], axis=0, keepdims=True)[:, :nh]
    bend = cend_ref[0] * (-LOG2E)
    nk = bend.shape[0]
    reach = 2.04 * jnp.sqrt(qn2 * kn2)
    blk = lax.broadcasted_iota(jnp.int32, (nk, nh), 0)
    rows = []
    for qi in range(qn2.shape[0]):
        d0 = qi * ndiag
        bprev = bend[d0 - 1:d0] if d0 > 0 else jnp.zeros((1, nh), F32)
        slack = 2.0 + 1e-5 * (jnp.abs(bprev) + jnp.abs(bend))
        live = (blk < d0) & (bprev - bend <= EXP2_ZERO_BELOW + reach[qi:qi + 1] + slack)
        rows.append(jnp.max(jnp.where(live, d0 - blk, 0), axis=0, keepdims=True))
    cnt_ref[0] = jnp.concatenate(rows, axis=0)


def _plan(qn2, kn2, cend, nh):
    b, nq, sub, _ = qn2.shape
    nt = kn2.shape[1]
    nk = cend.shape[1]
    return pl.pallas_call(
        functools.partial(_plan_kernel, nh=nh, ndiag=nk // nq),
        grid=(b,),
        in_specs=[
            pl.BlockSpec((1, nq, sub, LANES), lambda i: (i, 0, 0, 0)),
            pl.BlockSpec((1, nt, LANES), lambda i: (i, 0, 0)),
            pl.BlockSpec((1, nk, nh), lambda i: (i, 0, 0)),
        ],
        out_specs=pl.BlockSpec((1, nq, nh), lambda i: (i, 0, 0)),
        out_shape=jax.ShapeDtypeStruct((b, nq, nh), jnp.int32),
        compiler_params=_params(1),
        name="fox_plan",
    )(qn2, kn2, cend)


def _attn_cached_kernel(q_ref, kn_ref, vn_ref, pk_ref, pv_ref, cp_ref, cn_ref, o_ref):
    qh = _split_heads(q_ref[0])
    pkb = pk_ref[0].astype(BF16)
    pvb = pv_ref[0].astype(BF16)
    kn = kn_ref[0]
    vn = vn_ref[0]
    t = kn.shape[0]
    p_len = pkb.shape[0]
    rowi = lax.broadcasted_iota(jnp.int32, (t, t), 0)
    coli = lax.broadcasted_iota(jnp.int32, (t, t), 1)
    outs = []
    for hh in range(2):
        cp = cp_ref[0, 0, hh:hh + 1, :]
        tot = cp[:, p_len - 1:p_len]
        z1 = lax.dot_general(qh[hh], pkb, _NT, preferred_element_type=F32) - (cp - tot) * LOG2E
        z2 = (lax.dot_general(qh[hh], kn, _NT, preferred_element_type=F32)
              - cn_ref[0, 0, hh:hh + 1, :] * LOG2E)
        z2 = jnp.where(coli <= rowi, z2, NEG_BIG)
        m = jnp.maximum(jnp.max(z1, axis=1), jnp.max(z2, axis=1))[:, None]
        p1 = jnp.exp2(z1 - m)
        p2 = jnp.exp2(z2 - m)
        l = (jnp.sum(p1, axis=1) + jnp.sum(p2, axis=1))[:, None]
        o = (jnp.dot(p1.astype(BF16), pvb, preferred_element_type=F32)
             + jnp.dot(p2.astype(BF16), vn, preferred_element_type=F32))
        outs.append(o / l)
    lane = lax.broadcasted_iota(jnp.int32, (t, LANES), 1)
    o_ref[0] = jnp.where(lane < HEAD_DIM, outs[0], outs[1]).astype(o_ref.dtype)


def _attention_cached(qb, kb, vb, past_k, past_v, cum_past, cum_new):
    b, t, w = qb.shape
    p_len = past_k.shape[1]
    hp = w // LANES
    new_spec = pl.BlockSpec((1, t, LANES), lambda i, h: (i, 0, h))
    past_spec = pl.BlockSpec((1, p_len, LANES), lambda i, h: (i, 0, h))
    return pl.pallas_call(
        _attn_cached_kernel,
        grid=(b, hp),
        in_specs=[new_spec, new_spec, new_spec, past_spec, past_spec,
                  pl.BlockSpec((1, 1, 2, p_len), lambda i, h: (i, h, 0, 0)),
                  pl.BlockSpec((1, 1, 2, t), lambda i, h: (i, h, 0, 0))],
        out_specs=new_spec,
        out_shape=jax.ShapeDtypeStruct((b, t, w), BF16),
        compiler_params=_params(2),
        name="fox_attention_cached",
    )(qb, kb, vb, past_k, past_v,
      cum_past.reshape(b, hp, 2, p_len), cum_new.reshape(b, hp, 2, t))


def _fox_out_kernel(o_ref, sg_ref, x_ref, mod_ref, gpost_ref, w_ref, y_ref):
    yv = (o_ref[0].astype(F32) * sg_ref[0].astype(F32)).astype(BF16)
    y = jnp.dot(yv, w_ref[...], preferred_element_type=F32)
    y_ref[0] = _gated_residual(x_ref[0], y, mod_ref[0], gpost_ref[...])


def _fox_out(o, g, x, mod, mod_off, gpost, w_out, *, tt):
    b, t, d = x.shape
    w = o.shape[-1]
    return pl.pallas_call(
        _fox_out_kernel,
        grid=(b, t // tt),
        in_specs=[
            pl.BlockSpec((1, tt, w), lambda i, j: (i, j, 0)),
            pl.BlockSpec((1, tt, w), lambda i, j: (i, j, 0)),
            pl.BlockSpec((1, tt, d), lambda i, j: (i, j, 0)),
            pl.BlockSpec((1, 3, d), lambda i, j: (i + mod_off, 0, 0)),
            _const_spec((1, d)),
            _const_spec((w, d)),
        ],
        out_specs=pl.BlockSpec((1, tt, d), lambda i, j: (i, j, 0)),
        out_shape=jax.ShapeDtypeStruct((b, t, d), F32),
        compiler_params=_params(2),
        name="fox_out",
    )(o, g, x, mod, gpost, w_out)


def _tile(t, pref):
    return pref if t % pref == 0 else t


def kernel(x_prompt, x_sample, c_prompt, c_sample, state_lru_h, state_lru_conv, cache_fox_k, cache_fox_v, cache_fox_logf, norm_pre, norm_post, ada_w, ada_b, lru_w_in, lru_conv_w, lru_conv_b, lru_w_a, lru_b_a, lru_w_x, lru_b_x, lru_lambda, lru_w_out, fox_w_in, fox_b_f, fox_w_out):
    b, t, d = x_prompt.shape
    db, dt, _ = x_sample.shape
    depth = ada_w.shape[0]
    r = lru_lambda.shape[-1]
    nh = fox_b_f.shape[-1]
    w = fox_w_out.shape[1]
    p_len = cache_fox_k.shape[2]
    scale = float(HEAD_DIM) ** -0.5

    c_all = jnp.concatenate([c_prompt, c_sample], axis=0)
    mod = _adaln(c_all, ada_w, ada_b).reshape(depth, b + db, 3, d)

    xp, xs = x_prompt, x_sample

    j = 0
    w_ax = (0.5 * jnp.concatenate([lru_w_a[j], lru_w_x[j]], axis=-1)).astype(BF16)
    w_in_lru = jnp.concatenate([lru_w_in[j][:, :r], 0.5 * lru_w_in[j][:, r:]], axis=1).astype(BF16)
    lru_args = (norm_pre[0:1], norm_post[0:1], w_in_lru, lru_conv_w[j],
                lru_conv_b[j:j + 1], w_ax, 0.5 * lru_b_a[j:j + 1], 0.5 * lru_b_x[j:j + 1],
                lru_lambda[j:j + 1], lru_w_out[j].astype(BF16))
    xp, conv_p, h_p = _lru_layer(
        xp, mod[0], 0, *lru_args,
        jnp.zeros((b, CONV_TAPS - 1, r), F32), jnp.zeros((b, 1, r), F32), tt=_tile(t, 512))
    xs, conv_s, h_s = _lru_layer(
        xs, mod[0], b, *lru_args,
        state_lru_conv[j], state_lru_h[j].reshape(db, 1, r), tt=_tile(dt, 256))

    w_main = fox_w_in[j][:, :4 * w].astype(BF16)
    pad = jnp.zeros((d, LANES - 3 * nh), F32)
    w_f3 = jnp.concatenate([fox_w_in[j][:, 4 * w:]] * 3 + [pad], axis=1).astype(BF16)
    b_f3 = jnp.concatenate([fox_b_f[j]] * 3 + [jnp.zeros((LANES - 3 * nh,), F32)]).reshape(1, LANES)
    w_out = fox_w_out[j].astype(BF16)
    qscale = scale * LOG2E

    tt = _tile(t, 256)
    tk = _tile(t, 512)
    tq = tk
    qb, kb, k_p, v_p, sg_p, vt_p, lf_p, cum_p, aug_p, qn_p, kn_p = _fox_in(
        xp, mod[1], 0, norm_pre[1:2], w_main, w_f3, b_f3, nh, tt=tt, qscale=qscale)
    cnt = _plan(qn_p.reshape(b, t // tq, tq // tt, LANES), kn_p.reshape(b, t // tt, LANES),
                cum_p[:, tk - 1::tk, :], nh)
    o_p = _attention(cnt.transpose(0, 2, 1).reshape(-1), qb, kb, aug_p, vt_p, nh, tq=tq, tk=tk)
    yp = _fox_out(o_p, sg_p, xp, mod[1], 0, norm_post[1:2], w_out, tt=_tile(t, 512))

    qs, ksb, k_s, v_s, sg_s, _, lf_s, cum_s, _, _, _ = _fox_in(
        xs, mod[1], b, norm_pre[1:2], w_main, w_f3, b_f3, nh, tt=_tile(dt, 256), qscale=qscale)
    lf_past = cache_fox_logf[j].transpose(1, 0, 2).reshape(1, p_len, db * nh)
    cum_past = _cumsum_time(lf_past, tc=_tile(p_len, 512)).reshape(p_len, db, nh).transpose(1, 2, 0)
    o_s = _attention_cached(qs, ksb, v_s.astype(BF16),
                            cache_fox_k[j].reshape(db, p_len, w),
                            cache_fox_v[j].reshape(db, p_len, w),
                            cum_past, jnp.swapaxes(cum_s, 1, 2))
    ys = _fox_out(o_s, sg_s, xs, mod[1], b, norm_post[1:2], w_out, tt=_tile(dt, 512))

    hd = (nh, HEAD_DIM)
    return (yp, ys,
            h_p.reshape(1, b, r), conv_p[None],
            k_p.reshape(1, b, t, *hd), v_p.reshape(1, b, t, *hd), lf_p[None],
            h_s.reshape(1, db, r), conv_s[None],
            k_s.reshape(1, db, dt, *hd), v_s.reshape(1, db, dt, *hd), lf_s[None])
```

```python
import functools

import jax
import jax.numpy as jnp
from jax import lax
from jax.experimental import pallas as pl
from jax.experimental.pallas import tpu as pltpu

F32 = jnp.float32
BF16 = jnp.bfloat16

EPS = 1e-6
LRU_C = 8.0
CONV_TAPS = 4
HEAD_DIM = 64
LANES = 128
SUBLANES = 8
NEG_BIG = -1e30
LOG2E = 1.4426950408889634
ONES_ROWS = 16
VMEM_LIMIT_BYTES = 56 * 1024 * 1024

_NT = (((1,), (1,)), ((), ()))


def _params(n_grid):
    return pltpu.CompilerParams(
        dimension_semantics=("arbitrary",) * n_grid,
        vmem_limit_bytes=VMEM_LIMIT_BYTES)


def _const_spec(shape):
    nd = len(shape)
    return pl.BlockSpec(shape, lambda *_: (0,) * nd, pipeline_mode=pl.Buffered(1))


def _sigmoid(x):
    return 0.5 * jnp.tanh(0.5 * x) + 0.5


def _silu(x):
    return x * _sigmoid(x)


def _softplus(x):
    return jnp.maximum(x, 0.0) + jnp.log1p(jnp.exp(-jnp.abs(x)))


def _one_minus_exp_neg(t, exp_neg_t):
    p = jnp.full_like(t, -1.0 / 24.0)
    for c in (1.0 / 6.0, -0.5, 1.0):
        p = p * t + c
    return jnp.where(t < 1.0 / 64.0, t * p, 1.0 - exp_neg_t)


def _sqrt_pos(x):
    x = jnp.maximum(x, 1e-30)
    return x * lax.rsqrt(x)


def _modulated_norm(x, mod, g):
    ms = jnp.mean(x * x, axis=-1, keepdims=True)
    h = x * lax.rsqrt(ms + EPS) * g
    return h * (1.0 + mod[1:2]) + mod[0:1]


def _gated_residual(x, y, mod, g):
    ms = jnp.mean(y * y, axis=-1, keepdims=True)
    return x + mod[2:3] * (y * lax.rsqrt(ms + EPS) * g)


def _adaln_kernel(c_ref, w_ref, b_ref, o_ref):
    sc = _silu(c_ref[...]).astype(BF16)
    o_ref[0] = jnp.dot(sc, w_ref[0].astype(BF16), preferred_element_type=F32) + b_ref[0]


def _adaln(c_all, ada_w, ada_b):
    depth, d, d3 = ada_w.shape
    n = c_all.shape[0]
    nj = d3 // d
    return pl.pallas_call(
        _adaln_kernel,
        grid=(depth, nj),
        in_specs=[
            pl.BlockSpec((n, d), lambda i, j: (0, 0)),
            pl.BlockSpec((1, d, d), lambda i, j: (i, 0, j)),
            pl.BlockSpec((1, 1, d), lambda i, j: (i, 0, j)),
        ],
        out_specs=pl.BlockSpec((1, n, d), lambda i, j: (i, 0, j)),
        out_shape=jax.ShapeDtypeStruct((depth, n, d3), F32),
        compiler_params=_params(2),
        name="adaln",
    )(c_all, ada_w, ada_b.reshape(depth, 1, d3))


def _lru_kernel(x_ref, mod_ref, gpre_ref, gpost_ref, win_ref, cw_ref, cb_ref, wax_ref,
                ba_ref, bx_ref, lam_ref, wout_ref, conv0_ref, h0_ref,
                y_ref, convo_ref, hlast_ref,
                xbuf, xc_s, pre_a, pre_x, gate_s, yv_s, hcar, *, tt, r, nblk):
    t = pl.program_id(1)
    nt = pl.num_programs(1)
    tail0 = SUBLANES - (CONV_TAPS - 1)

    @pl.when(t == 0)
    def _():
        xbuf[tail0:SUBLANES, :] = conv0_ref[0]
        hcar[...] = h0_ref[0]

    mod = mod_ref[0]
    cw = cw_ref[...]
    half = tt // 2

    def front(lo):
        h = _modulated_norm(x_ref[0, lo:lo + half, :], mod, gpre_ref[...])
        z = jnp.dot(h.astype(BF16), win_ref[...], preferred_element_type=F32)
        xbuf[SUBLANES + lo:SUBLANES + lo + half, :] = z[:, :r]
        gate_s[lo:lo + half, :] = z[:, r:]
        xc = cb_ref[...] + xbuf[tail0 + lo:tail0 + lo + half, :] * cw[0:1]
        for k in range(1, CONV_TAPS):
            xc = xc + xbuf[tail0 + k + lo:tail0 + k + lo + half, :] * cw[k:k + 1]
        xc_s[lo:lo + half, :] = xc
        xcb = xc.astype(BF16)
        for n in range(nblk):
            c0, c1 = n * LANES, (n + 1) * LANES
            res = jnp.dot(xcb[:, c0:c1], wax_ref[n], preferred_element_type=F32)
            pre_a[lo:lo + half, c0:c1] = res[:, :LANES]
            pre_x[lo:lo + half, c0:c1] = res[:, LANES:]

    sp = LRU_C * _softplus(-lam_ref[...])
    c1 = (-0.5 * LOG2E) * sp
    c2 = sp
    ba = ba_ref[...]
    bx = bx_ref[...]
    row = lax.broadcasted_iota(jnp.int32, (SUBLANES, r), 0)

    def group(st, hprev):
        sl = slice(st, st + SUBLANES)
        th_a = jnp.tanh(pre_a[sl, :] + ba)
        th_x = jnp.tanh(pre_x[sl, :] + bx)
        a = jnp.exp2(c1 * th_a + c1)
        ig = 0.5 * th_x + 0.5
        u = _sqrt_pos(_one_minus_exp_neg(c2 * th_a + c2, a * a)) * (ig * xc_s[sl, :])
        for s in (1, 2, 4):
            a_sh = jnp.where(row >= s, pltpu.roll(a, s, axis=0), 1.0)
            u_sh = jnp.where(row >= s, pltpu.roll(u, s, axis=0), 0.0)
            u = a * u_sh + u
            a = a * a_sh
        hs = a * hprev + u
        gh = gate_s[sl, :]
        yv_s[sl, :] = hs * (gh * jnp.tanh(gh) + gh)
        return hs[SUBLANES - 1:SUBLANES, :]

    def scan(lo, hprev):
        for st in range(lo, lo + half, SUBLANES):
            hprev = group(st, hprev)
        return hprev

    def back(lo):
        y = jnp.dot(yv_s[lo:lo + half, :].astype(BF16), wout_ref[...], preferred_element_type=F32)
        y_ref[0, lo:lo + half, :] = _gated_residual(x_ref[0, lo:lo + half, :], y, mod, gpost_ref[...])

    front(0)
    front(half)
    tail = xbuf[tt + tail0:tt + SUBLANES, :]
    hmid = scan(0, hcar[...])
    back(0)
    hlast = scan(half, hmid)
    back(half)
    xbuf[tail0:SUBLANES, :] = tail
    hcar[...] = hlast

    @pl.when(t == nt - 1)
    def _():
        convo_ref[0] = tail
        hlast_ref[0] = hlast


def _lru_layer(x, mod, mod_off, gpre, gpost, w_in, conv_w, conv_b, w_ax, b_a, b_x, lam, w_out,
               conv0, h0, *, tt):
    b, t, d = x.shape
    r = lam.shape[-1]
    nblk = w_ax.shape[0]
    kern = functools.partial(_lru_kernel, tt=tt, r=r, nblk=nblk)
    return pl.pallas_call(
        kern,
        grid=(b, t // tt),
        in_specs=[
            pl.BlockSpec((1, tt, d), lambda i, j: (i, j, 0)),
            pl.BlockSpec((1, 3, d), lambda i, j: (i + mod_off, 0, 0)),
            _const_spec((1, d)), _const_spec((1, d)),
            _const_spec((d, 2 * r)),
            _const_spec((CONV_TAPS, r)), _const_spec((1, r)),
            _const_spec((nblk, LANES, 2 * LANES)),
            _const_spec((1, r)), _const_spec((1, r)), _const_spec((1, r)),
            _const_spec((r, d)),
            pl.BlockSpec((1, CONV_TAPS - 1, r), lambda i, j: (i, 0, 0)),
            pl.BlockSpec((1, 1, r), lambda i, j: (i, 0, 0)),
        ],
        out_specs=[
            pl.BlockSpec((1, tt, d), lambda i, j: (i, j, 0)),
            pl.BlockSpec((1, CONV_TAPS - 1, r), lambda i, j: (i, 0, 0)),
            pl.BlockSpec((1, 1, r), lambda i, j: (i, 0, 0)),
        ],
        out_shape=[
            jax.ShapeDtypeStruct((b, t, d), F32),
            jax.ShapeDtypeStruct((b, CONV_TAPS - 1, r), F32),
            jax.ShapeDtypeStruct((b, 1, r), F32),
        ],
        scratch_shapes=[
            pltpu.VMEM((tt + SUBLANES, r), F32),
            pltpu.VMEM((tt, r), F32), pltpu.VMEM((tt, r), F32), pltpu.VMEM((tt, r), F32),
            pltpu.VMEM((tt, r), F32), pltpu.VMEM((tt, r), F32),
            pltpu.VMEM((1, r), F32),
        ],
        compiler_params=_params(2),
        name="lru_layer",
    )(x, mod, gpre, gpost, w_in, conv_w, conv_b, w_ax, b_a, b_x, lam, w_out, conv0, h0)


def _split3(x):
    x1 = x.astype(BF16)
    r1 = x - x1.astype(F32)
    x2 = r1.astype(BF16)
    x3 = (r1 - x2.astype(F32)).astype(BF16)
    return x1, x2, x3


def _tri_cumsum(tri, x, carry):
    c = carry
    for piece in _split3(x):
        c = c + jnp.dot(tri, piece, preferred_element_type=F32)
    return c


def _fox_in_kernel(x_ref, mod_ref, gpre_ref, w_ref, wf_ref, bf_ref, tri_ref, hsel_ref,
                   q_ref, kb_ref, k_ref, v_ref, sg_ref, vt_ref, lf_ref, cum_ref, aug_ref,
                   qn_ref, kn_ref, car, *, w, nh, qscale):
    @pl.when(pl.program_id(1) == 0)
    def _():
        car[...] = jnp.zeros_like(car)

    hb = _modulated_norm(x_ref[0], mod_ref[0], gpre_ref[...]).astype(BF16)
    z = jnp.dot(hb, w_ref[...], preferred_element_type=F32)
    q = z[:, :w] * qscale
    q_ref[0] = q.astype(BF16)
    k = z[:, w:2 * w]
    k_ref[0] = k
    kb_ref[0] = k.astype(BF16)
    hsel = hsel_ref[...]
    qn_ref[0, 0] = jnp.max(jnp.dot((q * q).astype(BF16), hsel, preferred_element_type=F32),
                           axis=0, keepdims=True)
    kn_ref[0, 0] = jnp.max(jnp.dot((k * k).astype(BF16), hsel, preferred_element_type=F32),
                           axis=0, keepdims=True)
    v = z[:, 2 * w:3 * w]
    v_ref[0] = v
    sg_ref[0] = _silu(z[:, 3 * w:]).astype(BF16)
    tt = v.shape[0]
    vt = v.T.reshape(nh, HEAD_DIM, tt)
    vt_ref[0, :, 0] = jnp.concatenate([vt, jnp.ones((nh, ONES_ROWS, tt), F32)], axis=1).astype(BF16)
    fl = jnp.dot(hb, wf_ref[...], preferred_element_type=F32) + bf_ref[...]
    lf = -_softplus(-fl)
    c = _tri_cumsum(tri_ref[...], lf, car[...])
    tt = c.shape[0]
    car[...] = c[tt - 1:tt, :]
    lf_ref[0] = lf[:, :nh]
    cum_ref[0] = c[:, :nh]
    y1, y2, y3 = _split3(c * (-LOG2E))
    lane = lax.broadcasted_iota(jnp.int32, c.shape, 1)
    aug = jnp.where(lane < nh, y1.astype(F32),
                    jnp.where(lane < 2 * nh, y2.astype(F32),
                              jnp.where(lane < 3 * nh, y3.astype(F32), 0.0)))
    aug_ref[0] = aug.astype(BF16)


def _tri(n):
    return jnp.tril(jnp.ones((n, n), F32)).astype(BF16)


def _fox_in(x, mod, mod_off, gpre, w_main, w_f3, b_f3, nh, *, tt, qscale):
    b, t, d = x.shape
    w = w_main.shape[1] // 4
    rows = HEAD_DIM + ONES_ROWS
    kern = functools.partial(_fox_in_kernel, w=w, nh=nh, qscale=qscale)
    tile = lambda dt: jax.ShapeDtypeStruct((b, t, w), dt)
    tile_spec = pl.BlockSpec((1, tt, w), lambda i, j: (i, j, 0))
    head_spec = pl.BlockSpec((1, tt, nh), lambda i, j: (i, j, 0))
    norm_spec = pl.BlockSpec((1, 1, 1, LANES), lambda i, j: (i, j, 0, 0))
    hsel = (jnp.arange(w)[:, None] // HEAD_DIM == jnp.arange(LANES)[None, :]).astype(BF16)
    return pl.pallas_call(
        kern,
        grid=(b, t // tt),
        in_specs=[
            pl.BlockSpec((1, tt, d), lambda i, j: (i, j, 0)),
            pl.BlockSpec((1, 3, d), lambda i, j: (i + mod_off, 0, 0)),
            _const_spec((1, d)),
            _const_spec((d, 4 * w)),
            _const_spec((d, LANES)),
            _const_spec((1, LANES)),
            _const_spec((tt, tt)),
            _const_spec((w, LANES)),
        ],
        out_specs=[tile_spec] * 5 + [pl.BlockSpec((1, nh, 1, rows, tt), lambda i, j: (i, 0, j, 0, 0)),
                                     head_spec, head_spec,
                                     pl.BlockSpec((1, tt, LANES), lambda i, j: (i, j, 0)),
                                     norm_spec, norm_spec],
        out_shape=[tile(BF16), tile(BF16), tile(F32), tile(F32), tile(BF16),
                   jax.ShapeDtypeStruct((b, nh, t // tt, rows, tt), BF16),
                   jax.ShapeDtypeStruct((b, t, nh), F32), jax.ShapeDtypeStruct((b, t, nh), F32),
                   jax.ShapeDtypeStruct((b, t, LANES), BF16),
                   jax.ShapeDtypeStruct((b, t // tt, 1, LANES), F32),
                   jax.ShapeDtypeStruct((b, t // tt, 1, LANES), F32)],
        scratch_shapes=[pltpu.VMEM((1, LANES), F32)],
        compiler_params=_params(2),
        name="fox_in",
    )(x, mod, gpre, w_main, w_f3, b_f3, _tri(tt), hsel)


def _cumsum_kernel(x_ref, tri_ref, o_ref, car):
    @pl.when(pl.program_id(1) == 0)
    def _():
        car[...] = jnp.zeros_like(car)

    c = _tri_cumsum(tri_ref[...], x_ref[0], car[...])
    o_ref[0] = c
    tc = c.shape[0]
    car[...] = c[tc - 1:tc, :]


def _cumsum_time(x, *, tc):
    b, t, nh = x.shape
    return pl.pallas_call(
        _cumsum_kernel,
        grid=(b, t // tc),
        in_specs=[pl.BlockSpec((1, tc, nh), lambda i, j: (i, j, 0)), _const_spec((tc, tc))],
        out_specs=pl.BlockSpec((1, tc, nh), lambda i, j: (i, j, 0)),
        out_shape=jax.ShapeDtypeStruct((b, t, nh), F32),
        scratch_shapes=[pltpu.VMEM((1, nh), F32)],
        compiler_params=_params(2),
        name="cumsum_time",
    )(x, _tri(tc))


def _split_heads(q):
    qf = q.astype(F32)
    lane = lax.broadcasted_iota(jnp.int32, qf.shape, 1)
    return (jnp.where(lane < HEAD_DIM, qf, 0.0).astype(BF16),
            jnp.where(lane >= HEAD_DIM, qf, 0.0).astype(BF16))


def _attn_kernel(cnt_ref, q_ref, k_ref, a_ref, vt_ref, o_ref, z0, z1, zm0, zm1, p0, p1, al0, al1,
                 qa_s, m_s, acc_s, *, tq, tk, nh, tps):
    bi = pl.program_id(0)
    hp = pl.program_id(1)
    qs = pl.program_id(2)
    nq = pl.num_programs(2) * tps
    ns = 2 * tps
    z_s, zmax_s, p_s, alpha_s = (z0, z1), (zm0, zm1), (p0, p1), (al0, al1)

    @pl.when((bi == 0) & (hp == 0) & (qs == 0))
    def _():
        for p_ref, al_ref in zip(p_s, alpha_s):
            p_ref[...] = jnp.zeros_like(p_ref)
            al_ref[...] = jnp.ones_like(al_ref)

    lane = lax.broadcasted_iota(jnp.int32, (tq, LANES), 1)
    starts = [0]
    for g in range(ns):
        tile, hh = divmod(g, 2)
        head = 2 * hp + hh
        q = q_ref[0, tile * tq:(tile + 1) * tq, :].astype(F32)
        own = (lane >= hh * HEAD_DIM) & (lane < (hh + 1) * HEAD_DIM)
        pick = (lane == head) | (lane == head + nh) | (lane == head + 2 * nh)
        qa_s[g] = jnp.concatenate(
            [jnp.where(own, q, 0.0).astype(BF16), jnp.where(pick, 1.0, 0.0).astype(BF16)], axis=1)
        starts.append(starts[-1] + cnt_ref[(bi * nh + head) * nq + qs * tps + tile])
    n_off = starts[ns]
    last = ns + n_off - 1
    m_s[...] = jnp.full_like(m_s, NEG_BIG)
    acc_s[...] = jnp.zeros_like(acc_s)
    tv = vt_ref.shape[4]
    nsub = tk // tv

    def item(t):
        t = jnp.maximum(t, 0)
        u = t - ns
        g_off = sum((u >= starts[j]).astype(jnp.int32) for j in range(1, ns))
        start = sum(jnp.where(g_off == j, starts[j], 0) for j in range(1, ns))
        g = jnp.where(u < 0, t, g_off)
        back = jnp.where(u < 0, 0, u - start + 1)
        return g, g % 2, qs * tps + g // 2 - back

    def qk(t, slot, diagonal_stream=None):
        if diagonal_stream is None:
            g, _, blk = item(t)
        else:
            g, blk = diagonal_stream, qs * tps + diagonal_stream // 2
        st = pl.multiple_of(blk * tk, tk)
        keys = jnp.concatenate([k_ref[0, pl.ds(st, tk), :], a_ref[0, pl.ds(st, tk), :]], axis=1)
        z = lax.dot_general(keys, qa_s[g], _NT, preferred_element_type=F32)
        if diagonal_stream is not None:
            kpos = lax.broadcasted_iota(jnp.int32, z.shape, 0)
            qpos = lax.broadcasted_iota(jnp.int32, z.shape, 1)
            z = jnp.where(kpos <= qpos, z, NEG_BIG)
        z_s[slot][...] = z
        zmax_s[slot][...] = jnp.max(z, axis=0, keepdims=True)

    def softmax(t, slot):
        g, _, _ = item(t)
        m_prev = m_s[g]
        m_next = jnp.maximum(m_prev, zmax_s[slot][...])
        p_s[slot][...] = jnp.exp2(z_s[slot][...] - m_next).astype(BF16)
        alpha_s[slot][...] = jnp.exp2(m_prev - m_next)
        m_s[g] = m_next

    def pv(t, slot, valid):
        g, hh, blk = item(t)
        r = None
        for u in range(nsub):
            part = jnp.dot(vt_ref[0, hh, blk * nsub + u], p_s[slot][u * tv:(u + 1) * tv, :],
                           preferred_element_type=F32)
            r = part if r is None else r + part
        if valid is not None:
            r = jnp.where(valid, r, 0.0)
        acc_s[g] = acc_s[g] * alpha_s[slot][...] + r

    def iteration(t, slot, next_diagonal_stream=None):
        if not (isinstance(t, int) and t < 2):
            pv(t - 2, slot, None if isinstance(t, int) else t >= 2)
        softmax(t, slot)
        qk(t + 2, slot, next_diagonal_stream)

    def finish(slot):
        pv(last - 3, slot, last >= 3)
        softmax(last - 1, slot)
        pv(last - 2, 1 - slot, last >= 2)
        softmax(last, 1 - slot)
        pv(last - 1, slot, None)
        pv(last, 1 - slot, None)
        for tile in range(tps):
            outs = []
            for hh in range(2):
                acc = acc_s[2 * tile + hh]
                outs.append(acc[:HEAD_DIM] / acc[HEAD_DIM:HEAD_DIM + 1])
            o_ref[0, tile * tq:(tile + 1) * tq, :] = (
                jnp.concatenate(outs, axis=0).T.astype(o_ref.dtype))

    qk(0, 0, 0)
    qk(1, 1, 1)
    for t in range(ns - 2):
        iteration(t, t % 2, t + 2)
    first = ns - 2

    def pair(k, carry):
        iteration(first + 2 * k, 0)
        iteration(first + 2 * k + 1, 1)
        return carry

    lax.fori_loop(0, n_off // 2, pair, 0)
    odd = lax.rem(n_off, 2) == 1

    @pl.when(odd)
    def _():
        iteration(first + n_off - 1, 0)
        finish(1)

    @pl.when(jnp.logical_not(odd))
    def _():
        finish(0)


def _attention(cnt, qb, kb, aug, vt, nh, *, tq, tk):
    assert tq == tk, "one key block per query tile on the diagonal"
    b, t, w = qb.shape
    hp = w // LANES
    nq = t // tq
    tps = next(n for n in (4, 2, 1) if nq % n == 0)
    ns = 2 * tps
    _, _, nv, rows, tv = vt.shape
    kern = functools.partial(_attn_kernel, tq=tq, tk=tk, nh=nh, tps=tps)
    grid_spec = pltpu.PrefetchScalarGridSpec(
        num_scalar_prefetch=1,
        grid=(b, hp, nq // tps),
        in_specs=[
            pl.BlockSpec((1, tps * tq, LANES), lambda i, h, j, c: (i, j, h)),
            pl.BlockSpec((1, t, LANES), lambda i, h, j, c: (i, 0, h)),
            pl.BlockSpec((1, t, LANES), lambda i, h, j, c: (i, 0, 0)),
            pl.BlockSpec((1, 2, nv, rows, tv), lambda i, h, j, c: (i, h, 0, 0, 0)),
        ],
        out_specs=pl.BlockSpec((1, tps * tq, LANES), lambda i, h, j, c: (i, j, h)),
        scratch_shapes=(
            [pltpu.VMEM((tk, tq), F32)] * 2
            + [pltpu.VMEM((1, tq), F32)] * 2
            + [pltpu.VMEM((tk, tq), BF16)] * 2
            + [pltpu.VMEM((1, tq), F32)] * 2
            + [pltpu.VMEM((ns, tq, 2 * LANES), BF16),
               pltpu.VMEM((ns, 1, tq), F32),
               pltpu.VMEM((ns, rows, tq), F32)]))
    return pl.pallas_call(
        kern,
        grid_spec=grid_spec,
        out_shape=jax.ShapeDtypeStruct((b, t, w), BF16),
        compiler_params=_params(3),
        name="fox_attention",
    )(cnt, qb, kb, aug, vt)


EXP2_ZERO_BELOW = 150.0


def _plan_kernel(qn_ref, kn_ref, cend_ref, cnt_ref, *, nh, ndiag):
    qn2 = jnp.max(qn_ref[0], axis=1)[:, :nh]
    kn2 = jnp.max(kn_ref[0], axis=0, keepdims=True)[:, :nh]
    bend = cend_ref[0] * (-LOG2E)
    nk = bend.shape[0]
    reach = 2.04 * jnp.sqrt(qn2 * kn2)
    blk = lax.broadcasted_iota(jnp.int32, (nk, nh), 0)
    rows = []
    for qi in range(qn2.shape[0]):
        d0 = qi * ndiag
        bprev = bend[d0 - 1:d0] if d0 > 0 else jnp.zeros((1, nh), F32)
        slack = 2.0 + 1e-5 * (jnp.abs(bprev) + jnp.abs(bend))
        live = (blk < d0) & (bprev - bend <= EXP2_ZERO_BELOW + reach[qi:qi + 1] + slack)
        rows.append(jnp.max(jnp.where(live, d0 - blk, 0), axis=0, keepdims=True))
    cnt_ref[0] = jnp.concatenate(rows, axis=0)


def _plan(qn2, kn2, cend, nh):
    b, nq, sub, _ = qn2.shape
    nt = kn2.shape[1]
    nk = cend.shape[1]
    return pl.pallas_call(
        functools.partial(_plan_kernel, nh=nh, ndiag=nk // nq),
        grid=(b,),
        in_specs=[
            pl.BlockSpec((1, nq, sub, LANES), lambda i: (i, 0, 0, 0)),
            pl.BlockSpec((1, nt, LANES), lambda i: (i, 0, 0)),
            pl.BlockSpec((1, nk, nh), lambda i: (i, 0, 0)),
        ],
        out_specs=pl.BlockSpec((1, nq, nh), lambda i: (i, 0, 0)),
        out_shape=jax.ShapeDtypeStruct((b, nq, nh), jnp.int32),
        compiler_params=_params(1),
        name="fox_plan",
    )(qn2, kn2, cend)


def _attn_cached_kernel(q_ref, kn_ref, vn_ref, pk_ref, pv_ref, cp_ref, cn_ref, o_ref):
    qq = jnp.concatenate(_split_heads(q_ref[0]), axis=0)
    pkb = pk_ref[0].astype(BF16)
    pvb = pv_ref[0].astype(BF16)
    kn = kn_ref[0]
    vn = vn_ref[0]
    t = kn.shape[0]
    p_len = pkb.shape[0]
    s1 = lax.dot_general(qq, pkb, _NT, preferred_element_type=F32)
    s2 = lax.dot_general(qq, kn, _NT, preferred_element_type=F32)
    rowi = lax.broadcasted_iota(jnp.int32, (t, t), 0)
    coli = lax.broadcasted_iota(jnp.int32, (t, t), 1)
    p1s, p2s, ls = [], [], []
    for hh in range(2):
        rows = slice(hh * t, (hh + 1) * t)
        cp = cp_ref[0, 0, hh:hh + 1, :]
        tot = cp[:, p_len - 1:p_len]
        z1 = s1[rows] - (cp - tot) * LOG2E
        z2 = s2[rows] - cn_ref[0, 0, hh:hh + 1, :] * LOG2E
        z2 = jnp.where(coli <= rowi, z2, NEG_BIG)
        m = jnp.maximum(jnp.max(z1, axis=1), jnp.max(z2, axis=1))[:, None]
        p1 = jnp.exp2(z1 - m)
        p2 = jnp.exp2(z2 - m)
        ls.append((jnp.sum(p1, axis=1) + jnp.sum(p2, axis=1))[:, None])
        p1s.append(p1.astype(BF16))
        p2s.append(p2.astype(BF16))
    o = (jnp.dot(jnp.concatenate(p1s, axis=0), pvb, preferred_element_type=F32)
         + jnp.dot(jnp.concatenate(p2s, axis=0), vn, preferred_element_type=F32))
    lane = lax.broadcasted_iota(jnp.int32, (t, LANES), 1)
    o_ref[0] = jnp.where(lane < HEAD_DIM, o[:t] / ls[0], o[t:] / ls[1]).astype(o_ref.dtype)


def _attention_cached(qb, kb, vb, past_k, past_v, cum_past, cum_new):
    b, t, w = qb.shape
    p_len = past_k.shape[1]
    hp = w // LANES
    new_spec = pl.BlockSpec((1, t, LANES), lambda i, h: (i, 0, h))
    past_spec = pl.BlockSpec((1, p_len, LANES), lambda i, h: (i, 0, h))
    return pl.pallas_call(
        _attn_cached_kernel,
        grid=(b, hp),
        in_specs=[new_spec, new_spec, new_spec, past_spec, past_spec,
                  pl.BlockSpec((1, 1, 2, p_len), lambda i, h: (i, h, 0, 0)),
                  pl.BlockSpec((1, 1, 2, t), lambda i, h: (i, h, 0, 0))],
        out_specs=new_spec,
        out_shape=jax.ShapeDtypeStruct((b, t, w), BF16),
        compiler_params=_params(2),
        name="fox_attention_cached",
    )(qb, kb, vb, past_k, past_v,
      cum_past.reshape(b, hp, 2, p_len), cum_new.reshape(b, hp, 2, t))


def _fox_out_kernel(o_ref, sg_ref, x_ref, mod_ref, gpost_ref, w_ref, y_ref):
    yv = (o_ref[0].astype(F32) * sg_ref[0].astype(F32)).astype(BF16)
    y = jnp.dot(yv, w_ref[...], preferred_element_type=F32)
    y_ref[0] = _gated_residual(x_ref[0], y, mod_ref[0], gpost_ref[...])


def _fox_out(o, g, x, mod, mod_off, gpost, w_out, *, tt):
    b, t, d = x.shape
    w = o.shape[-1]
    return pl.pallas_call(
        _fox_out_kernel,
        grid=(b, t // tt),
        in_specs=[
            pl.BlockSpec((1, tt, w), lambda i, j: (i, j, 0)),
            pl.BlockSpec((1, tt, w), lambda i, j: (i, j, 0)),
            pl.BlockSpec((1, tt, d), lambda i, j: (i, j, 0)),
            pl.BlockSpec((1, 3, d), lambda i, j: (i + mod_off, 0, 0)),
            _const_spec((1, d)),
            _const_spec((w, d)),
        ],
        out_specs=pl.BlockSpec((1, tt, d), lambda i, j: (i, j, 0)),
        out_shape=jax.ShapeDtypeStruct((b, t, d), F32),
        compiler_params=_params(2),
        name="fox_out",
    )(o, g, x, mod, gpost, w_out)


def _tile(t, pref):
    return pref if t % pref == 0 else t


def kernel(x_prompt, x_sample, c_prompt, c_sample, state_lru_h, state_lru_conv, cache_fox_k, cache_fox_v, cache_fox_logf, norm_pre, norm_post, ada_w, ada_b, lru_w_in, lru_conv_w, lru_conv_b, lru_w_a, lru_b_a, lru_w_x, lru_b_x, lru_lambda, lru_w_out, fox_w_in, fox_b_f, fox_w_out):
    b, t, d = x_prompt.shape
    db, dt, _ = x_sample.shape
    depth = ada_w.shape[0]
    r = lru_lambda.shape[-1]
    nh = fox_b_f.shape[-1]
    w = fox_w_out.shape[1]
    p_len = cache_fox_k.shape[2]
    scale = float(HEAD_DIM) ** -0.5

    c_all = jnp.concatenate([c_prompt, c_sample], axis=0)
    mod = _adaln(c_all, ada_w, ada_b).reshape(depth, b + db, 3, d)

    xp, xs = x_prompt, x_sample

    j = 0
    w_ax = (0.5 * jnp.concatenate([lru_w_a[j], lru_w_x[j]], axis=-1)).astype(BF16)
    w_in_lru = jnp.concatenate([lru_w_in[j][:, :r], 0.5 * lru_w_in[j][:, r:]], axis=1).astype(BF16)
    lru_args = (norm_pre[0:1], norm_post[0:1], w_in_lru, lru_conv_w[j],
                lru_conv_b[j:j + 1], w_ax, 0.5 * lru_b_a[j:j + 1], 0.5 * lru_b_x[j:j + 1],
                lru_lambda[j:j + 1], lru_w_out[j].astype(BF16))
    xp, conv_p, h_p = _lru_layer(
        xp, mod[0], 0, *lru_args,
        jnp.zeros((b, CONV_TAPS - 1, r), F32), jnp.zeros((b, 1, r), F32), tt=_tile(t, 512))
    xs, conv_s, h_s = _lru_layer(
        xs, mod[0], b, *lru_args,
        state_lru_conv[j], state_lru_h[j].reshape(db, 1, r), tt=_tile(dt, 256))

    w_main = fox_w_in[j][:, :4 * w].astype(BF16)
    pad = jnp.zeros((d, LANES - 3 * nh), F32)
    w_f3 = jnp.concatenate([fox_w_in[j][:, 4 * w:]] * 3 + [pad], axis=1).astype(BF16)
    b_f3 = jnp.concatenate([fox_b_f[j]] * 3 + [jnp.zeros((LANES - 3 * nh,), F32)]).reshape(1, LANES)
    w_out = fox_w_out[j].astype(BF16)
    qscale = scale * LOG2E

    tt = _tile(t, 256)
    tk = _tile(t, 512)
    tq = tk
    qb, kb, k_p, v_p, sg_p, vt_p, lf_p, cum_p, aug_p, qn_p, kn_p = _fox_in(
        xp, mod[1], 0, norm_pre[1:2], w_main, w_f3, b_f3, nh, tt=tt, qscale=qscale)
    cnt = _plan(qn_p.reshape(b, t // tq, tq // tt, LANES), kn_p.reshape(b, t // tt, LANES),
                cum_p[:, tk - 1::tk, :], nh)
    o_p = _attention(cnt.transpose(0, 2, 1).reshape(-1), qb, kb, aug_p, vt_p, nh, tq=tq, tk=tk)
    yp = _fox_out(o_p, sg_p, xp, mod[1], 0, norm_post[1:2], w_out, tt=_tile(t, 512))

    qs, ksb, k_s, v_s, sg_s, _, lf_s, cum_s, _, _, _ = _fox_in(
        xs, mod[1], b, norm_pre[1:2], w_main, w_f3, b_f3, nh, tt=_tile(dt, 256), qscale=qscale)
    lf_past = cache_fox_logf[j].transpose(1, 0, 2).reshape(1, p_len, db * nh)
    cum_past = _cumsum_time(lf_past, tc=_tile(p_len, 512)).reshape(p_len, db, nh).transpose(1, 2, 0)
    o_s = _attention_cached(qs, ksb, v_s.astype(BF16),
                            cache_fox_k[j].reshape(db, p_len, w),
                            cache_fox_v[j].reshape(db, p_len, w),
                            cum_past, jnp.swapaxes(cum_s, 1, 2))
    ys = _fox_out(o_s, sg_s, xs, mod[1], b, norm_post[1:2], w_out, tt=_tile(dt, 512))

    hd = (nh, HEAD_DIM)
    return (yp, ys,
            h_p.reshape(1, b, r), conv_p[None],
            k_p.reshape(1, b, t, *hd), v_p.reshape(1, b, t, *hd), lf_p[None],
            h_s.reshape(1, db, r), conv_s[None],
            k_s.reshape(1, db, dt, *hd), v_s.reshape(1, db, dt, *hd), lf_s[None])
```

```python
import functools

import jax
import jax.numpy as jnp
from jax import lax
from jax.experimental import pallas as pl
from jax.experimental.pallas import tpu as pltpu

F32 = jnp.float32
BF16 = jnp.bfloat16

EPS = 1e-6
LRU_C = 8.0
CONV_TAPS = 4
HEAD_DIM = 64
LANES = 128
SUBLANES = 8
NEG_BIG = -1e30
LOG2E = 1.4426950408889634
ONES_ROWS = 16
VMEM_LIMIT_BYTES = 56 * 1024 * 1024

_NT = (((1,), (1,)), ((), ()))


def _params(n_grid):
    return pltpu.CompilerParams(
        dimension_semantics=("arbitrary",) * n_grid,
        vmem_limit_bytes=VMEM_LIMIT_BYTES)


def _const_spec(shape):
    nd = len(shape)
    return pl.BlockSpec(shape, lambda *_: (0,) * nd, pipeline_mode=pl.Buffered(1))


def _sigmoid(x):
    return 0.5 * jnp.tanh(0.5 * x) + 0.5


def _silu(x):
    return x * _sigmoid(x)


def _softplus(x):
    return jnp.maximum(x, 0.0) + jnp.log1p(jnp.exp(-jnp.abs(x)))


def _one_minus_exp_neg(t, exp_neg_t):
    p = jnp.full_like(t, -1.0 / 24.0)
    for c in (1.0 / 6.0, -0.5, 1.0):
        p = p * t + c
    return jnp.where(t < 1.0 / 64.0, t * p, 1.0 - exp_neg_t)


def _sqrt_pos(x):
    x = jnp.maximum(x, 1e-30)
    return x * lax.rsqrt(x)


def _modulated_norm(x, mod, g):
    ms = jnp.mean(x * x, axis=-1, keepdims=True)
    h = x * lax.rsqrt(ms + EPS) * g
    return h * (1.0 + mod[1:2]) + mod[0:1]


def _gated_residual(x, y, mod, g):
    ms = jnp.mean(y * y, axis=-1, keepdims=True)
    return x + mod[2:3] * (y * lax.rsqrt(ms + EPS) * g)


def _adaln_kernel(c_ref, w_ref, b_ref, o_ref):
    sc = _silu(c_ref[...]).astype(BF16)
    o_ref[0] = jnp.dot(sc, w_ref[0].astype(BF16), preferred_element_type=F32) + b_ref[0]


def _adaln(c_all, ada_w, ada_b):
    depth, d, d3 = ada_w.shape
    n = c_all.shape[0]
    nj = d3 // d
    return pl.pallas_call(
        _adaln_kernel,
        grid=(depth, nj),
        in_specs=[
            pl.BlockSpec((n, d), lambda i, j: (0, 0)),
            pl.BlockSpec((1, d, d), lambda i, j: (i, 0, j)),
            pl.BlockSpec((1, 1, d), lambda i, j: (i, 0, j)),
        ],
        out_specs=pl.BlockSpec((1, n, d), lambda i, j: (i, 0, j)),
        out_shape=jax.ShapeDtypeStruct((depth, n, d3), F32),
        compiler_params=_params(2),
        name="adaln",
    )(c_all, ada_w, ada_b.reshape(depth, 1, d3))


def _lru_kernel(x_ref, mod_ref, gpre_ref, gpost_ref, win_ref, cw_ref, cb_ref, wax_ref,
                ba_ref, bx_ref, lam_ref, wout_ref, conv0_ref, h0_ref,
                y_ref, convo_ref, hlast_ref,
                xbuf, xc_s, pre_a, pre_x, gate_s, yv_s, hcar, *, tt, r, nblk):
    t = pl.program_id(1)
    nt = pl.num_programs(1)
    tail0 = SUBLANES - (CONV_TAPS - 1)

    @pl.when(t == 0)
    def _():
        xbuf[tail0:SUBLANES, :] = conv0_ref[0]
        hcar[...] = h0_ref[0]

    mod = mod_ref[0]
    cw = cw_ref[...]
    half = tt // 2

    def front(lo):
        h = _modulated_norm(x_ref[0, lo:lo + half, :], mod, gpre_ref[...])
        z = jnp.dot(h.astype(BF16), win_ref[...], preferred_element_type=F32)
        xbuf[SUBLANES + lo:SUBLANES + lo + half, :] = z[:, :r]
        gate_s[lo:lo + half, :] = z[:, r:]
        xc = cb_ref[...] + xbuf[tail0 + lo:tail0 + lo + half, :] * cw[0:1]
        for k in range(1, CONV_TAPS):
            xc = xc + xbuf[tail0 + k + lo:tail0 + k + lo + half, :] * cw[k:k + 1]
        xc_s[lo:lo + half, :] = xc
        xcb = xc.astype(BF16)
        for n in range(nblk):
            c0, c1 = n * LANES, (n + 1) * LANES
            res = jnp.dot(xcb[:, c0:c1], wax_ref[n], preferred_element_type=F32)
            pre_a[lo:lo + half, c0:c1] = res[:, :LANES]
            pre_x[lo:lo + half, c0:c1] = res[:, LANES:]

    sp = LRU_C * _softplus(-lam_ref[...])
    c1 = (-0.5 * LOG2E) * sp
    c2 = sp
    ba = ba_ref[...]
    bx = bx_ref[...]
    row = lax.broadcasted_iota(jnp.int32, (SUBLANES, r), 0)

    def group(st, hprev):
        sl = slice(st, st + SUBLANES)
        th_a = jnp.tanh(pre_a[sl, :] + ba)
        th_x = jnp.tanh(pre_x[sl, :] + bx)
        a = jnp.exp2(c1 * th_a + c1)
        ig = 0.5 * th_x + 0.5
        u = _sqrt_pos(_one_minus_exp_neg(c2 * th_a + c2, a * a)) * (ig * xc_s[sl, :])
        for s in (1, 2, 4):
            a_sh = jnp.where(row >= s, pltpu.roll(a, s, axis=0), 1.0)
            u_sh = jnp.where(row >= s, pltpu.roll(u, s, axis=0), 0.0)
            u = a * u_sh + u
            a = a * a_sh
        hs = a * hprev + u
        gh = gate_s[sl, :]
        yv_s[sl, :] = hs * (gh * jnp.tanh(gh) + gh)
        return hs[SUBLANES - 1:SUBLANES, :]

    def scan(lo, hprev):
        for st in range(lo, lo + half, SUBLANES):
            hprev = group(st, hprev)
        return hprev

    def back(lo):
        y = jnp.dot(yv_s[lo:lo + half, :].astype(BF16), wout_ref[...], preferred_element_type=F32)
        y_ref[0, lo:lo + half, :] = _gated_residual(x_ref[0, lo:lo + half, :], y, mod, gpost_ref[...])

    front(0)
    front(half)
    tail = xbuf[tt + tail0:tt + SUBLANES, :]
    hmid = scan(0, hcar[...])
    back(0)
    hlast = scan(half, hmid)
    back(half)
    xbuf[tail0:SUBLANES, :] = tail
    hcar[...] = hlast

    @pl.when(t == nt - 1)
    def _():
        convo_ref[0] = tail
        hlast_ref[0] = hlast


def _lru_layer(x, mod, mod_off, gpre, gpost, w_in, conv_w, conv_b, w_ax, b_a, b_x, lam, w_out,
               conv0, h0, *, tt):
    b, t, d = x.shape
    r = lam.shape[-1]
    nblk = w_ax.shape[0]
    kern = functools.partial(_lru_kernel, tt=tt, r=r, nblk=nblk)
    return pl.pallas_call(
        kern,
        grid=(b, t // tt),
        in_specs=[
            pl.BlockSpec((1, tt, d), lambda i, j: (i, j, 0)),
            pl.BlockSpec((1, 3, d), lambda i, j: (i + mod_off, 0, 0)),
            _const_spec((1, d)), _const_spec((1, d)),
            _const_spec((d, 2 * r)),
            _const_spec((CONV_TAPS, r)), _const_spec((1, r)),
            _const_spec((nblk, LANES, 2 * LANES)),
            _const_spec((1, r)), _const_spec((1, r)), _const_spec((1, r)),
            _const_spec((r, d)),
            pl.BlockSpec((1, CONV_TAPS - 1, r), lambda i, j: (i, 0, 0)),
            pl.BlockSpec((1, 1, r), lambda i, j: (i, 0, 0)),
        ],
        out_specs=[
            pl.BlockSpec((1, tt, d), lambda i, j: (i, j, 0)),
            pl.BlockSpec((1, CONV_TAPS - 1, r), lambda i, j: (i, 0, 0)),
            pl.BlockSpec((1, 1, r), lambda i, j: (i, 0, 0)),
        ],
        out_shape=[
            jax.ShapeDtypeStruct((b, t, d), F32),
            jax.ShapeDtypeStruct((b, CONV_TAPS - 1, r), F32),
            jax.ShapeDtypeStruct((b, 1, r), F32),
        ],
        scratch_shapes=[
            pltpu.VMEM((tt + SUBLANES, r), F32),
            pltpu.VMEM((tt, r), F32), pltpu.VMEM((tt, r), F32), pltpu.VMEM((tt, r), F32),
            pltpu.VMEM((tt, r), F32), pltpu.VMEM((tt, r), F32),
            pltpu.VMEM((1, r), F32),
        ],
        compiler_params=_params(2),
        name="lru_layer",
    )(x, mod, gpre, gpost, w_in, conv_w, conv_b, w_ax, b_a, b_x, lam, w_out, conv0, h0)


def _split3(x):
    x1 = x.astype(BF16)
    r1 = x - x1.astype(F32)
    x2 = r1.astype(BF16)
    x3 = (r1 - x2.astype(F32)).astype(BF16)
    return x1, x2, x3


def _tri_cumsum(tri, x, carry):
    c = carry
    for piece in _split3(x):
        c = c + jnp.dot(tri, piece, preferred_element_type=F32)
    return c


def _fox_in_kernel(x_ref, mod_ref, gpre_ref, w_ref, wf_ref, bf_ref, tri_ref, hsel_ref,
                   q_ref, kb_ref, k_ref, v_ref, sg_ref, vt_ref, lf_ref, cum_ref, aug_ref,
                   qn_ref, kn_ref, car, *, w, nh, qscale):
    @pl.when(pl.program_id(1) == 0)
    def _():
        car[...] = jnp.zeros_like(car)

    hb = _modulated_norm(x_ref[0], mod_ref[0], gpre_ref[...]).astype(BF16)
    z = jnp.dot(hb, w_ref[...], preferred_element_type=F32)
    q = z[:, :w] * qscale
    q_ref[0] = q.astype(BF16)
    k = z[:, w:2 * w]
    k_ref[0] = k
    kb_ref[0] = k.astype(BF16)
    hsel = hsel_ref[...]
    qn_ref[0, 0] = jnp.max(jnp.dot((q * q).astype(BF16), hsel, preferred_element_type=F32),
                           axis=0, keepdims=True)
    kn_ref[0, 0] = jnp.max(jnp.dot((k * k).astype(BF16), hsel, preferred_element_type=F32),
                           axis=0, keepdims=True)
    v = z[:, 2 * w:3 * w]
    v_ref[0] = v
    sg_ref[0] = _silu(z[:, 3 * w:]).astype(BF16)
    tt = v.shape[0]
    vt = v.T.reshape(nh, HEAD_DIM, tt)
    vt_ref[0, :, 0] = jnp.concatenate([vt, jnp.ones((nh, ONES_ROWS, tt), F32)], axis=1).astype(BF16)
    fl = jnp.dot(hb, wf_ref[...], preferred_element_type=F32) + bf_ref[...]
    lf = -_softplus(-fl)
    c = _tri_cumsum(tri_ref[...], lf, car[...])
    tt = c.shape[0]
    car[...] = c[tt - 1:tt, :]
    lf_ref[0] = lf[:, :nh]
    cum_ref[0] = c[:, :nh]
    y1, y2, y3 = _split3(c * (-LOG2E))
    lane = lax.broadcasted_iota(jnp.int32, c.shape, 1)
    aug = jnp.where(lane < nh, y1.astype(F32),
                    jnp.where(lane < 2 * nh, y2.astype(F32),
                              jnp.where(lane < 3 * nh, y3.astype(F32), 0.0)))
    aug_ref[0] = aug.astype(BF16)


def _tri(n):
    return jnp.tril(jnp.ones((n, n), F32)).astype(BF16)


def _fox_in(x, mod, mod_off, gpre, w_main, w_f3, b_f3, nh, *, tt, qscale):
    b, t, d = x.shape
    w = w_main.shape[1] // 4
    rows = HEAD_DIM + ONES_ROWS
    kern = functools.partial(_fox_in_kernel, w=w, nh=nh, qscale=qscale)
    tile = lambda dt: jax.ShapeDtypeStruct((b, t, w), dt)
    tile_spec = pl.BlockSpec((1, tt, w), lambda i, j: (i, j, 0))
    head_spec = pl.BlockSpec((1, tt, nh), lambda i, j: (i, j, 0))
    norm_spec = pl.BlockSpec((1, 1, 1, LANES), lambda i, j: (i, j, 0, 0))
    hsel = (jnp.arange(w)[:, None] // HEAD_DIM == jnp.arange(LANES)[None, :]).astype(BF16)
    return pl.pallas_call(
        kern,
        grid=(b, t // tt),
        in_specs=[
            pl.BlockSpec((1, tt, d), lambda i, j: (i, j, 0)),
            pl.BlockSpec((1, 3, d), lambda i, j: (i + mod_off, 0, 0)),
            _const_spec((1, d)),
            _const_spec((d, 4 * w)),
            _const_spec((d, LANES)),
            _const_spec((1, LANES)),
            _const_spec((tt, tt)),
            _const_spec((w, LANES)),
        ],
        out_specs=[tile_spec] * 5 + [pl.BlockSpec((1, nh, 1, rows, tt), lambda i, j: (i, 0, j, 0, 0)),
                                     head_spec, head_spec,
                                     pl.BlockSpec((1, tt, LANES), lambda i, j: (i, j, 0)),
                                     norm_spec, norm_spec],
        out_shape=[tile(BF16), tile(BF16), tile(F32), tile(F32), tile(BF16),
                   jax.ShapeDtypeStruct((b, nh, t // tt, rows, tt), BF16),
                   jax.ShapeDtypeStruct((b, t, nh), F32), jax.ShapeDtypeStruct((b, t, nh), F32),
                   jax.ShapeDtypeStruct((b, t, LANES), BF16),
                   jax.ShapeDtypeStruct((b, t // tt, 1, LANES), F32),
                   jax.ShapeDtypeStruct((b, t // tt, 1, LANES), F32)],
        scratch_shapes=[pltpu.VMEM((1, LANES), F32)],
        compiler_params=_params(2),
        name="fox_in",
    )(x, mod, gpre, w_main, w_f3, b_f3, _tri(tt), hsel)


def _cumsum_kernel(x_ref, tri_ref, o_ref, car):
    @pl.when(pl.program_id(1) == 0)
    def _():
        car[...] = jnp.zeros_like(car)

    c = _tri_cumsum(tri_ref[...], x_ref[0], car[...])
    o_ref[0] = c
    tc = c.shape[0]
    car[...] = c[tc - 1:tc, :]


def _cumsum_time(x, *, tc):
    b, t, nh = x.shape
    return pl.pallas_call(
        _cumsum_kernel,
        grid=(b, t // tc),
        in_specs=[pl.BlockSpec((1, tc, nh), lambda i, j: (i, j, 0)), _const_spec((tc, tc))],
        out_specs=pl.BlockSpec((1, tc, nh), lambda i, j: (i, j, 0)),
        out_shape=jax.ShapeDtypeStruct((b, t, nh), F32),
        scratch_shapes=[pltpu.VMEM((1, nh), F32)],
        compiler_params=_params(2),
        name="cumsum_time",
    )(x, _tri(tc))


def _split_heads(q):
    qf = q.astype(F32)
    lane = lax.broadcasted_iota(jnp.int32, qf.shape, 1)
    return (jnp.where(lane < HEAD_DIM, qf, 0.0).astype(BF16),
            jnp.where(lane >= HEAD_DIM, qf, 0.0).astype(BF16))


def _attn_kernel(cnt_ref, q_ref, k_ref, a_ref, vt_ref, o_ref, z0, z1, zm0, zm1, p0, p1, al0, al1,
                 qa_s, m_s, acc_s, *, tq, tk, nh, tps):
    bi = pl.program_id(0)
    hp = pl.program_id(1)
    qs = pl.program_id(2)
    nq = pl.num_programs(2) * tps
    ns = 2 * tps
    z_s, zmax_s, p_s, alpha_s = (z0, z1), (zm0, zm1), (p0, p1), (al0, al1)

    @pl.when((bi == 0) & (hp == 0) & (qs == 0))
    def _():
        for p_ref, al_ref in zip(p_s, alpha_s):
            p_ref[...] = jnp.zeros_like(p_ref)
            al_ref[...] = jnp.ones_like(al_ref)

    lane = lax.broadcasted_iota(jnp.int32, (tq, LANES), 1)
    starts = [0]
    for g in range(ns):
        tile, hh = divmod(g, 2)
        head = 2 * hp + hh
        q = q_ref[0, tile * tq:(tile + 1) * tq, :].astype(F32)
        own = (lane >= hh * HEAD_DIM) & (lane < (hh + 1) * HEAD_DIM)
        pick = (lane == head) | (lane == head + nh) | (lane == head + 2 * nh)
        qa_s[g] = jnp.concatenate(
            [jnp.where(own, q, 0.0).astype(BF16), jnp.where(pick, 1.0, 0.0).astype(BF16)], axis=1)
        starts.append(starts[-1] + cnt_ref[(bi * nh + head) * nq + qs * tps + tile])
    n_off = starts[ns]
    last = ns + n_off - 1
    m_s[...] = jnp.full_like(m_s, NEG_BIG)
    acc_s[...] = jnp.zeros_like(acc_s)
    tv = vt_ref.shape[4]
    nsub = tk // tv

    def item(t):
        t = jnp.maximum(t, 0)
        u = t - ns
        g_off = sum((u >= starts[j]).astype(jnp.int32) for j in range(1, ns))
        start = sum(jnp.where(g_off == j, starts[j], 0) for j in range(1, ns))
        g = jnp.where(u < 0, t, g_off)
        back = jnp.where(u < 0, 0, u - start + 1)
        return g, g % 2, qs * tps + g // 2 - back

    def qk(t, slot, diagonal_stream=None):
        if diagonal_stream is None:
            g, _, blk = item(t)
        else:
            g, blk = diagonal_stream, qs * tps + diagonal_stream // 2
        st = pl.multiple_of(blk * tk, tk)
        keys = jnp.concatenate([k_ref[0, pl.ds(st, tk), :], a_ref[0, pl.ds(st, tk), :]], axis=1)
        z = lax.dot_general(keys, qa_s[g], _NT, preferred_element_type=F32)
        if diagonal_stream is not None:
            kpos = lax.broadcasted_iota(jnp.int32, z.shape, 0)
            qpos = lax.broadcasted_iota(jnp.int32, z.shape, 1)
            z = jnp.where(kpos <= qpos, z, NEG_BIG)
        z_s[slot][...] = z
        zmax_s[slot][...] = jnp.max(z, axis=0, keepdims=True)

    def softmax(t, slot):
        g, _, _ = item(t)
        m_prev = m_s[g]
        m_next = jnp.maximum(m_prev, zmax_s[slot][...])
        p_s[slot][...] = jnp.exp2(z_s[slot][...] - m_next).astype(BF16)
        alpha_s[slot][...] = jnp.exp2(m_prev - m_next)
        m_s[g] = m_next

    def pv(t, slot, valid):
        g, hh, blk = item(t)
        r = None
        for u in range(nsub):
            part = jnp.dot(vt_ref[0, hh, blk * nsub + u], p_s[slot][u * tv:(u + 1) * tv, :],
                           preferred_element_type=F32)
            r = part if r is None else r + part
        if valid is not None:
            r = jnp.where(valid, r, 0.0)
        acc_s[g] = acc_s[g] * alpha_s[slot][...] + r

    def iteration(t, slot, next_diagonal_stream=None):
        if not (isinstance(t, int) and t < 2):
            pv(t - 2, slot, None if isinstance(t, int) else t >= 2)
        softmax(t, slot)
        qk(t + 2, slot, next_diagonal_stream)

    def finish(slot):
        pv(last - 3, slot, last >= 3)
        softmax(last - 1, slot)
        pv(last - 2, 1 - slot, last >= 2)
        softmax(last, 1 - slot)
        pv(last - 1, slot, None)
        pv(last, 1 - slot, None)
        for tile in range(tps):
            outs = []
            for hh in range(2):
                acc = acc_s[2 * tile + hh]
                outs.append(acc[:HEAD_DIM] / acc[HEAD_DIM:HEAD_DIM + 1])
            o_ref[0, tile * tq:(tile + 1) * tq, :] = (
                jnp.concatenate(outs, axis=0).T.astype(o_ref.dtype))

    qk(0, 0, 0)
    qk(1, 1, 1)
    for t in range(ns - 2):
        iteration(t, t % 2, t + 2)
    first = ns - 2

    def pair(k, carry):
        iteration(first + 2 * k, 0)
        iteration(first + 2 * k + 1, 1)
        return carry

    lax.fori_loop(0, n_off // 2, pair, 0)
    odd = lax.rem(n_off, 2) == 1

    @pl.when(odd)
    def _():
        iteration(first + n_off - 1, 0)
        finish(1)

    @pl.when(jnp.logical_not(odd))
    def _():
        finish(0)


def _attention(cnt, qb, kb, aug, vt, nh, *, tq, tk):
    assert tq == tk, "one key block per query tile on the diagonal"
    b, t, w = qb.shape
    hp = w // LANES
    nq = t // tq
    tps = next(n for n in (4, 2, 1) if nq % n == 0)
    ns = 2 * tps
    _, _, nv, rows, tv = vt.shape
    kern = functools.partial(_attn_kernel, tq=tq, tk=tk, nh=nh, tps=tps)
    grid_spec = pltpu.PrefetchScalarGridSpec(
        num_scalar_prefetch=1,
        grid=(b, hp, nq // tps),
        in_specs=[
            pl.BlockSpec((1, tps * tq, LANES), lambda i, h, j, c: (i, j, h)),
            pl.BlockSpec((1, t, LANES), lambda i, h, j, c: (i, 0, h)),
            pl.BlockSpec((1, t, LANES), lambda i, h, j, c: (i, 0, 0)),
            pl.BlockSpec((1, 2, nv, rows, tv), lambda i, h, j, c: (i, h, 0, 0, 0)),
        ],
        out_specs=pl.BlockSpec((1, tps * tq, LANES), lambda i, h, j, c: (i, j, h)),
        scratch_shapes=(
            [pltpu.VMEM((tk, tq), F32)] * 2
            + [pltpu.VMEM((1, tq), F32)] * 2
            + [pltpu.VMEM((tk, tq), BF16)] * 2
            + [pltpu.VMEM((1, tq), F32)] * 2
            + [pltpu.VMEM((ns, tq, 2 * LANES), BF16),
               pltpu.VMEM((ns, 1, tq), F32),
               pltpu.VMEM((ns, rows, tq), F32)]))
    return pl.pallas_call(
        kern,
        grid_spec=grid_spec,
        out_shape=jax.ShapeDtypeStruct((b, t, w), BF16),
        compiler_params=_params(3),
        name="fox_attention",
    )(cnt, qb, kb, aug, vt)


EXP2_ZERO_BELOW = 150.0


def _plan_kernel(qn_ref, kn_ref, cend_ref, cnt_ref, *, nh, ndiag):
    qn2 = jnp.max(qn_ref[0], axis=1)[:, :nh]
    kn2 = jnp.max(kn_ref[0], axis=0, keepdims=True)[:, :nh]
    bend = cend_ref[0] * (-LOG2E)
    nk = bend.shape[0]
    reach = 2.04 * jnp.sqrt(qn2 * kn2)
    blk = lax.broadcasted_iota(jnp.int32, (nk, nh), 0)
    rows = []
    for qi in range(qn2.shape[0]):
        d0 = qi * ndiag
        bprev = bend[d0 - 1:d0] if d0 > 0 else jnp.zeros((1, nh), F32)
        slack = 2.0 + 1e-5 * (jnp.abs(bprev) + jnp.abs(bend))
        live = (blk < d0) & (bprev - bend <= EXP2_ZERO_BELOW + reach[qi:qi + 1] + slack)
        rows.append(jnp.max(jnp.where(live, d0 - blk, 0), axis=0, keepdims=True))
    cnt_ref[0] = jnp.concatenate(rows, axis=0)


def _plan(qn2, kn2, cend, nh):
    b, nq, sub, _ = qn2.shape
    nt = kn2.shape[1]
    nk = cend.shape[1]
    return pl.pallas_call(
        functools.partial(_plan_kernel, nh=nh, ndiag=nk // nq),
        grid=(b,),
        in_specs=[
            pl.BlockSpec((1, nq, sub, LANES), lambda i: (i, 0, 0, 0)),
            pl.BlockSpec((1, nt, LANES), lambda i: (i, 0, 0)),
            pl.BlockSpec((1, nk, nh), lambda i: (i, 0, 0)),
        ],
        out_specs=pl.BlockSpec((1, nq, nh), lambda i: (i, 0, 0)),
        out_shape=jax.ShapeDtypeStruct((b, nq, nh), jnp.int32),
        compiler_params=_params(1),
        name="fox_plan",
    )(qn2, kn2, cend)


def _attn_cached_kernel(q_ref, kn_ref, vn_ref, pk_ref, pv_ref, cp_ref, cn_ref, o_ref):
    qq = jnp.concatenate(_split_heads(q_ref[0]), axis=0)
    pkb = pk_ref[0].astype(BF16)
    pvb = pv_ref[0].astype(BF16)
    kn = kn_ref[0]
    vn = vn_ref[0]
    t = kn.shape[0]
    p_len = pkb.shape[0]
    s1 = lax.dot_general(qq, pkb, _NT, preferred_element_type=F32)
    s2 = lax.dot_general(qq, kn, _NT, preferred_element_type=F32)
    rowi = lax.broadcasted_iota(jnp.int32, (t, t), 0)
    coli = lax.broadcasted_iota(jnp.int32, (t, t), 1)
    p1s, p2s, ls = [], [], []
    for hh in range(2):
        rows = slice(hh * t, (hh + 1) * t)
        cp = cp_ref[0, 0, hh:hh + 1, :]
        tot = cp[:, p_len - 1:p_len]
        z1 = s1[rows] - (cp - tot) * LOG2E
        z2 = s2[rows] - cn_ref[0, 0, hh:hh + 1, :] * LOG2E
        z2 = jnp.where(coli <= rowi, z2, NEG_BIG)
        m = jnp.maximum(jnp.max(z1, axis=1), jnp.max(z2, axis=1))[:, None]
        p1 = jnp.exp2(z1 - m)
        p2 = jnp.exp2(z2 - m)
        ls.append((jnp.sum(p1, axis=1) + jnp.sum(p2, axis=1))[:, None])
        p1s.append(p1.astype(BF16))
        p2s.append(p2.astype(BF16))
    o = (jnp.dot(jnp.concatenate(p1s, axis=0), pvb, preferred_element_type=F32)
         + jnp.dot(jnp.concatenate(p2s, axis=0), vn, preferred_element_type=F32))
    lane = lax.broadcasted_iota(jnp.int32, (t, LANES), 1)
    o_ref[0] = jnp.where(lane < HEAD_DIM, o[:t] / ls[0], o[t:] / ls[1]).astype(o_ref.dtype)


def _attention_cached(qb, kb, vb, past_k, past_v, cum_past, cum_new):
    b, t, w = qb.shape
    p_len = past_k.shape[1]
    hp = w // LANES
    new_spec = pl.BlockSpec((1, t, LANES), lambda i, h: (i, 0, h))
    past_spec = pl.BlockSpec((1, p_len, LANES), lambda i, h: (i, 0, h))
    return pl.pallas_call(
        _attn_cached_kernel,
        grid=(b, hp),
        in_specs=[new_spec, new_spec, new_spec, past_spec, past_spec,
                  pl.BlockSpec((1, 1, 2, p_len), lambda i, h: (i, h, 0, 0)),
                  pl.BlockSpec((1, 1, 2, t), lambda i, h: (i, h, 0, 0))],
        out_specs=new_spec,
        out_shape=jax.ShapeDtypeStruct((b, t, w), BF16),
        compiler_params=_params(2),
        name="fox_attention_cached",
    )(qb, kb, vb, past_k, past_v,
      cum_past.reshape(b, hp, 2, p_len), cum_new.reshape(b, hp, 2, t))


def _fox_out_kernel(o_ref, sg_ref, x_ref, mod_ref, gpost_ref, w_ref, y_ref):
    yv = (o_ref[0].astype(F32) * sg_ref[0].astype(F32)).astype(BF16)
    y = jnp.dot(yv, w_ref[...], preferred_element_type=F32)
    y_ref[0] = _gated_residual(x_ref[0], y, mod_ref[0], gpost_ref[...])


def _fox_out(o, g, x, mod, mod_off, gpost, w_out, *, tt):
    b, t, d = x.shape
    w = o.shape[-1]
    return pl.pallas_call(
        _fox_out_kernel,
        grid=(b, t // tt),
        in_specs=[
            pl.BlockSpec((1, tt, w), lambda i, j: (i, j, 0)),
            pl.BlockSpec((1, tt, w), lambda i, j: (i, j, 0)),
            pl.BlockSpec((1, tt, d), lambda i, j: (i, j, 0)),
            pl.BlockSpec((1, 3, d), lambda i, j: (i + mod_off, 0, 0)),
            _const_spec((1, d)),
            _const_spec((w, d)),
        ],
        out_specs=pl.BlockSpec((1, tt, d), lambda i, j: (i, j, 0)),
        out_shape=jax.ShapeDtypeStruct((b, t, d), F32),
        compiler_params=_params(2),
        name="fox_out",
    )(o, g, x, mod, gpost, w_out)


def _tile(t, pref):
    return pref if t % pref == 0 else t


def kernel(x_prompt, x_sample, c_prompt, c_sample, state_lru_h, state_lru_conv, cache_fox_k, cache_fox_v, cache_fox_logf, norm_pre, norm_post, ada_w, ada_b, lru_w_in, lru_conv_w, lru_conv_b, lru_w_a, lru_b_a, lru_w_x, lru_b_x, lru_lambda, lru_w_out, fox_w_in, fox_b_f, fox_w_out):
    b, t, d = x_prompt.shape
    db, dt, _ = x_sample.shape
    depth = ada_w.shape[0]
    r = lru_lambda.shape[-1]
    nh = fox_b_f.shape[-1]
    w = fox_w_out.shape[1]
    p_len = cache_fox_k.shape[2]
    scale = float(HEAD_DIM) ** -0.5

    c_all = jnp.concatenate([c_prompt, c_sample], axis=0)
    mod = _adaln(c_all, ada_w, ada_b).reshape(depth, b + db, 3, d)

    xp, xs = x_prompt, x_sample

    j = 0
    w_ax = (0.5 * jnp.concatenate([lru_w_a[j], lru_w_x[j]], axis=-1)).astype(BF16)
    w_in_lru = jnp.concatenate([lru_w_in[j][:, :r], 0.5 * lru_w_in[j][:, r:]], axis=1).astype(BF16)
    lru_args = (norm_pre[0:1], norm_post[0:1], w_in_lru, lru_conv_w[j],
                lru_conv_b[j:j + 1], w_ax, 0.5 * lru_b_a[j:j + 1], 0.5 * lru_b_x[j:j + 1],
                lru_lambda[j:j + 1], lru_w_out[j].astype(BF16))
    xp, conv_p, h_p = _lru_layer(
        xp, mod[0], 0, *lru_args,
        jnp.zeros((b, CONV_TAPS - 1, r), F32), jnp.zeros((b, 1, r), F32), tt=_tile(t, 512))
    xs, conv_s, h_s = _lru_layer(
        xs, mod[0], b, *lru_args,
        state_lru_conv[j], state_lru_h[j].reshape(db, 1, r), tt=_tile(dt, 256))

    w_main = fox_w_in[j][:, :4 * w].astype(BF16)
    pad = jnp.zeros((d, LANES - 3 * nh), F32)
    w_f3 = jnp.concatenate([fox_w_in[j][:, 4 * w:]] * 3 + [pad], axis=1).astype(BF16)
    b_f3 = jnp.concatenate([fox_b_f[j]] * 3 + [jnp.zeros((LANES - 3 * nh,), F32)]).reshape(1, LANES)
    w_out = fox_w_out[j].astype(BF16)
    qscale = scale * LOG2E

    tt = _tile(t, 256)
    tk = _tile(t, 512)
    tq = tk
    qb, kb, k_p, v_p, sg_p, vt_p, lf_p, cum_p, aug_p, qn_p, kn_p = _fox_in(
        xp, mod[1], 0, norm_pre[1:2], w_main, w_f3, b_f3, nh, tt=tt, qscale=qscale)
    cnt = _plan(qn_p.reshape(b, t // tq, tq // tt, LANES), kn_p.reshape(b, t // tt, LANES),
                cum_p[:, tk - 1::tk, :], nh)
    o_p = _attention(cnt.transpose(0, 2, 1).reshape(-1), qb, kb, aug_p, vt_p, nh, tq=tq, tk=tk)
    yp = _fox_out(o_p, sg_p, xp, mod[1], 0, norm_post[1:2], w_out, tt=_tile(t, 1024))

    qs, ksb, k_s, v_s, sg_s, _, lf_s, cum_s, _, _, _ = _fox_in(
        xs, mod[1], b, norm_pre[1:2], w_main, w_f3, b_f3, nh, tt=_tile(dt, 256), qscale=qscale)
    lf_past = cache_fox_logf[j].transpose(1, 0, 2).reshape(1, p_len, db * nh)
    cum_past = _cumsum_time(lf_past, tc=_tile(p_len, 512)).reshape(p_len, db, nh).transpose(1, 2, 0)
    o_s = _attention_cached(qs, ksb, v_s.astype(BF16),
                            cache_fox_k[j].reshape(db, p_len, w),
                            cache_fox_v[j].reshape(db, p_len, w),
                            cum_past, jnp.swapaxes(cum_s, 1, 2))
    ys = _fox_out(o_s, sg_s, xs, mod[1], b, norm_post[1:2], w_out, tt=_tile(dt, 512))

    hd = (nh, HEAD_DIM)
    return (yp, ys,
            h_p.reshape(1, b, r), conv_p[None],
            k_p.reshape(1, b, t, *hd), v_p.reshape(1, b, t, *hd), lf_p[None],
            h_s.reshape(1, db, r), conv_s[None],
            k_s.reshape(1, db, dt, *hd), v_s.reshape(1, db, dt, *hd), lf_s[None])
```
